```python
import math
import jax, jax.numpy as jnp
from jax import lax
import numpy as np

D_MODEL = 1024
BATCH = 4
SEQ = 4096
DEPTH = 4
DEC_BATCH = 128
DEC_SEQ = 1
PAST_LEN = 2048
PAGE_SIZE = 128

N_META = 16
N_MIXERS = 2
N_ATTN_LAYERS = (DEPTH + 1) // 2
N_CONV_LAYERS = DEPTH // 2
N_HEADS = 8
HEAD_DIM = 64
V_HEAD_DIM = 2 * HEAD_DIM
Q_BLOCK = 128
CONV_WIDTH = 31
CONV_STATE = CONV_WIDTH - 1
PEER_HEADS = 8
N_KEYS = 128
N_EXPERTS = N_KEYS * N_KEYS
PEER_TOPK = 16
PEER_DK = 256
PEER_HALF = PEER_DK // 2
PEER_BLOCK = 256
RMS_EPS = 1e-6
LN_EPS = 1e-5
POOL_NUM = 5
POOL_DEN = 4

kernel_name = "diffattn_conformer_peer_hybrid_step"


def rmsnorm(x, g):
    xf = x.astype(jnp.float32)
    y = xf * lax.rsqrt(jnp.mean(xf * xf, axis=-1, keepdims=True) + RMS_EPS)
    return (y * g.astype(jnp.float32)).astype(x.dtype)


def lambda_full(lq1, lk1, lq2, lk2, lam_init):
    f = jnp.float32
    return (jnp.exp(jnp.sum(lq1.astype(f) * lk1.astype(f)))
            - jnp.exp(jnp.sum(lq2.astype(f) * lk2.astype(f))) + lam_init)


def qkv_split(h, w_qkv):
    b, t, _ = h.shape
    q, k, v = jnp.split(h @ w_qkv, 3, axis=-1)
    q = q.reshape(b, t, N_HEADS, 2, HEAD_DIM)
    k = k.reshape(b, t, N_HEADS, 2, HEAD_DIM)
    v = v.reshape(b, t, N_HEADS, V_HEAD_DIM)
    return q, k, v


def diff_combine(s, lam):
    p = jax.nn.softmax(s, axis=-1)
    return p[:, :, 0] - lam * p[:, :, 1]


def diff_out(o, subln_g, lam_init, w_o):
    b, t = o.shape[:2]
    o = rmsnorm(o, subln_g) * (1.0 - lam_init)
    return o.reshape(b, t, N_HEADS * V_HEAD_DIM) @ w_o


def diff_attn_prompt(h, w_qkv, lq1, lk1, lq2, lk2, subln_g, w_o, lam_init):
    b, L, _ = h.shape
    q, k, v = qkv_split(h, w_qkv)
    lam = lambda_full(lq1, lk1, lq2, lk2, lam_init)
    n_blk = -(-L // Q_BLOCK)
    Lp = n_blk * Q_BLOCK
    qp = jnp.pad(q, ((0, 0), (0, Lp - L), (0, 0), (0, 0), (0, 0)))
    qb = jnp.moveaxis(qp.reshape(b, n_blk, Q_BLOCK, N_HEADS, 2, HEAD_DIM), 1, 0)
    k_pos = jnp.arange(L)
    scale = HEAD_DIM ** -0.5

    def block(args):
        qi, start = args
        s = jnp.einsum('bqhcd,bkhcd->bhcqk', qi, k).astype(jnp.float32) * scale
        q_pos = start + jnp.arange(Q_BLOCK)
        s = jnp.where(k_pos[None, :] <= q_pos[:, None], s, -jnp.inf)
        a = diff_combine(s, lam)
        return jnp.einsum('bhqk,bkhe->bqhe', a.astype(v.dtype), v)

    o = lax.map(block, (qb, jnp.arange(n_blk) * Q_BLOCK))
    o = jnp.moveaxis(o, 0, 1).reshape(b, Lp, N_HEADS, V_HEAD_DIM)[:, :L]
    y = diff_out(o, subln_g, lam_init, w_o)
    return y, k.reshape(b, L, N_HEADS, 2 * HEAD_DIM), v


def diff_attn_sample(h, k_pool, v_pool, page_table, w_qkv, lq1, lk1, lq2, lk2, subln_g, w_o, lam_init):
    b, S, _ = h.shape
    q, k, v = qkv_split(h, w_qkv)
    lam = lambda_full(lq1, lk1, lq2, lk2, lam_init)
    past = page_table.shape[1] * PAGE_SIZE
    k_past = k_pool[page_table].reshape(b, past, N_HEADS, 2, HEAD_DIM)
    v_past = v_pool[page_table].reshape(b, past, N_HEADS, V_HEAD_DIM)
    scale = HEAD_DIM ** -0.5
    s_past = jnp.einsum('bqhcd,bkhcd->bhcqk', q, k_past).astype(jnp.float32) * scale
    s_new = jnp.einsum('bqhcd,bkhcd->bhcqk', q, k).astype(jnp.float32) * scale
    causal = jnp.arange(S)[None, :] <= jnp.arange(S)[:, None]
    s_new = jnp.where(causal, s_new, -jnp.inf)
    a = diff_combine(jnp.concatenate([s_past, s_new], axis=-1), lam).astype(v.dtype)
    o = (jnp.einsum('bhqk,bkhe->bqhe', a[..., :past], v_past)
         + jnp.einsum('bhqk,bkhe->bqhe', a[..., past:], v))
    y = diff_out(o, subln_g, lam_init, w_o)
    return y, k.reshape(b, S, N_HEADS, 2 * HEAD_DIM), v


def conv_glu(h, w_pw1, b_pw1):
    a, g = jnp.split(h @ w_pw1 + b_pw1, 2, axis=-1)
    return a * jax.nn.sigmoid(g)


def conv_tail(u_full, w_dw, b_dw, ln_g, ln_b, w_pw2):
    c = lax.conv_general_dilated(u_full, w_dw[:, None, :], window_strides=(1,), padding='VALID',
                                 dimension_numbers=('NWC', 'WIO', 'NWC'),
                                 feature_group_count=D_MODEL) + b_dw
    cf = c.astype(jnp.float32)
    mu = jnp.mean(cf, axis=-1, keepdims=True)
    var = jnp.mean(jnp.square(cf - mu), axis=-1, keepdims=True)
    cn = ((cf - mu) * lax.rsqrt(var + LN_EPS) * ln_g.astype(jnp.float32)
          + ln_b.astype(jnp.float32)).astype(c.dtype)
    return jax.nn.silu(cn) @ w_pw2


def peer_ffn(h, w_query, sub_keys, u_emb, v_emb):
    b, t, d = h.shape
    n = b * t
    n_blk = -(-n // PEER_BLOCK)
    xt = jnp.pad(h.reshape(n, d), ((0, n_blk * PEER_BLOCK - n), (0, 0))).reshape(n_blk, PEER_BLOCK, d)

    def block(xb):
        q = (xb @ w_query).reshape(PEER_BLOCK, PEER_HEADS, 2, PEER_HALF)
        s = jnp.einsum('thcd,hckd->thck', q, sub_keys).astype(jnp.float32)
        sv, si = lax.top_k(s, PEER_TOPK)
        cand_s = (sv[:, :, 0, :, None] + sv[:, :, 1, None, :]).reshape(PEER_BLOCK, PEER_HEADS, PEER_TOPK * PEER_TOPK)
        cand_i = (si[:, :, 0, :, None] * N_KEYS + si[:, :, 1, None, :]).reshape(PEER_BLOCK, PEER_HEADS, PEER_TOPK * PEER_TOPK)
        top_s, pos = lax.top_k(cand_s, PEER_TOPK)
        idx = jnp.take_along_axis(cand_i, pos, axis=-1)
        g = jax.nn.softmax(top_s, axis=-1)
        act = jax.nn.gelu(jnp.einsum('thkd,td->thk', u_emb[idx], xb), approximate=False)
        w = (g * act.astype(jnp.float32)).astype(xb.dtype)
        return jnp.einsum('thk,thkd->td', w, v_emb[idx])

    out = lax.map(block, xt).reshape(n_blk * PEER_BLOCK, d)[:n]
    return out.reshape(b, t, d)


def _normal(k, shape, scale):
    return jax.random.normal(k, shape, jnp.float32) * scale


def setup_inputs(seed: int = 0) -> dict:
    key = jax.random.key(seed)
    ks = jax.random.split(key, 32)
    n_pages = PAST_LEN // PAGE_SIZE
    n_pool = (DEC_BATCH * n_pages * POOL_NUM) // POOL_DEN
    D = D_MODEL
    page_table = jax.random.permutation(ks[5], n_pool)[:DEC_BATCH * n_pages].reshape(DEC_BATCH, n_pages).astype(jnp.int32)
    return {
        'x_prompt': _normal(ks[0], (BATCH, SEQ, D), 1.0),
        'x_sample': _normal(ks[1], (DEC_BATCH, DEC_SEQ, D), 1.0),
        'cache_k': _normal(ks[2], (N_ATTN_LAYERS, n_pool, PAGE_SIZE, N_HEADS, 2 * HEAD_DIM), 1.0),
        'cache_v': _normal(ks[3], (N_ATTN_LAYERS, n_pool, PAGE_SIZE, N_HEADS, 2 * HEAD_DIM), 1.0),
        'state_conv': _normal(ks[4], (N_CONV_LAYERS, DEC_BATCH, CONV_STATE, D), 0.5),
        'page_table': page_table,
        'meta_tokens': _normal(ks[6], (N_META, D), 1.0),
        'g_mix': 1.0 + _normal(ks[7], (DEPTH, D), 0.02),
        'g_ffn': 1.0 + _normal(ks[8], (DEPTH, D), 0.02),
        'g_final': 1.0 + _normal(ks[9], (D,), 0.02),
        'w_qkv': _normal(ks[10], (N_ATTN_LAYERS, D, 3 * D), D ** -0.5),
        'lambda_q1': _normal(ks[11], (N_ATTN_LAYERS, HEAD_DIM), 0.1),
        'lambda_k1': _normal(ks[12], (N_ATTN_LAYERS, HEAD_DIM), 0.1),
        'lambda_q2': _normal(ks[13], (N_ATTN_LAYERS, HEAD_DIM), 0.1),
        'lambda_k2': _normal(ks[14], (N_ATTN_LAYERS, HEAD_DIM), 0.1),
        'subln_g': 1.0 + _normal(ks[15], (N_ATTN_LAYERS, V_HEAD_DIM), 0.02),
        'w_o_attn': _normal(ks[16], (N_ATTN_LAYERS, D, D), D ** -0.5),
        'w_pw1': _normal(ks[17], (N_CONV_LAYERS, D, 2 * D), D ** -0.5),
        'b_pw1': _normal(ks[18], (N_CONV_LAYERS, 2 * D), 0.02),
        'w_dw': _normal(ks[19], (N_CONV_LAYERS, CONV_WIDTH, D), CONV_WIDTH ** -0.5),
        'b_dw': _normal(ks[20], (N_CONV_LAYERS, D), 0.02),
        'ln_g': 1.0 + _normal(ks[21], (N_CONV_LAYERS, D), 0.02),
        'ln_b': _normal(ks[22], (N_CONV_LAYERS, D), 0.02),
        'w_pw2': _normal(ks[23], (N_CONV_LAYERS, D, D), D ** -0.5),
        'peer_w_query': _normal(ks[24], (DEPTH, D, PEER_HEADS * PEER_DK), D ** -0.5),
        'peer_sub_keys': _normal(ks[25], (DEPTH, PEER_HEADS, 2, N_KEYS, PEER_HALF), PEER_HALF ** -0.5),
        'peer_u': _normal(ks[26], (DEPTH, N_EXPERTS, D), D ** -0.5),
        'peer_v': _normal(ks[27], (DEPTH, N_EXPERTS, D), 0.1),
    }


def reference(x_prompt, x_sample, cache_k, cache_v, state_conv, page_table, meta_tokens,
              g_mix, g_ffn, g_final, w_qkv, lambda_q1, lambda_k1, lambda_q2, lambda_k2,
              subln_g, w_o_attn, w_pw1, b_pw1, w_dw, b_dw, ln_g, ln_b, w_pw2,
              peer_w_query, peer_sub_keys, peer_u, peer_v):
    b = x_prompt.shape[0]
    meta = jnp.broadcast_to(meta_tokens.astype(x_prompt.dtype)[None], (b, N_META, D_MODEL))
    xp = jnp.concatenate([meta, x_prompt], axis=1)
    xs = x_sample
    kp_l, vp_l, cp_l, ks_l, vs_l, cs_l = [], [], [], [], [], []
    for i in range(DEPTH):
        hp = rmsnorm(xp, g_mix[i])
        hs = rmsnorm(xs, g_mix[i])
        j = i // N_MIXERS
        if i % N_MIXERS == 0:
            lam_init = 0.8 - 0.6 * math.exp(-0.3 * i)
            yp, kp, vp = diff_attn_prompt(hp, w_qkv[j], lambda_q1[j], lambda_k1[j], lambda_q2[j],
                                          lambda_k2[j], subln_g[j], w_o_attn[j], lam_init)
            ys, kn, vn = diff_attn_sample(hs, cache_k[j], cache_v[j], page_table, w_qkv[j],
                                          lambda_q1[j], lambda_k1[j], lambda_q2[j], lambda_k2[j],
                                          subln_g[j], w_o_attn[j], lam_init)
            kp_l.append(kp); vp_l.append(vp); ks_l.append(kn); vs_l.append(vn)
        else:
            up = conv_glu(hp, w_pw1[j], b_pw1[j])
            up_full = jnp.pad(up, ((0, 0), (CONV_STATE, 0), (0, 0)))
            us = conv_glu(hs, w_pw1[j], b_pw1[j])
            us_full = jnp.concatenate([state_conv[j].astype(us.dtype), us], axis=1)
            yp = conv_tail(up_full, w_dw[j], b_dw[j], ln_g[j], ln_b[j], w_pw2[j])
            ys = conv_tail(us_full, w_dw[j], b_dw[j], ln_g[j], ln_b[j], w_pw2[j])
            cp_l.append(up_full[:, -CONV_STATE:])
            cs_l.append(us_full[:, -CONV_STATE:])
        xp = xp + yp
        xs = xs + ys
        xp = xp + peer_ffn(rmsnorm(xp, g_ffn[i]), peer_w_query[i], peer_sub_keys[i], peer_u[i], peer_v[i])
        xs = xs + peer_ffn(rmsnorm(xs, g_ffn[i]), peer_w_query[i], peer_sub_keys[i], peer_u[i], peer_v[i])
    y_prompt = rmsnorm(xp, g_final)[:, N_META:]
    y_sample = rmsnorm(xs, g_final)
    k_prompt = jnp.stack(kp_l)
    v_prompt = jnp.stack(vp_l)
    conv_prompt = jnp.stack(cp_l)
    k_sample = jnp.stack(ks_l)
    v_sample = jnp.stack(vs_l)
    conv_sample = jnp.stack(cs_l)
    return (y_prompt, y_sample, k_prompt, v_prompt, conv_prompt, k_sample, v_sample, conv_sample)
```

```python
import functools
import math

import jax
import jax.numpy as jnp
from jax import lax
from jax.experimental import pallas as pl
from jax.experimental.pallas import tpu as pltpu

F32 = jnp.float32
BF16 = jnp.bfloat16

D_MODEL = 1024
N_META = 16
N_HEADS = 8
HEAD_DIM = 64
V_HEAD_DIM = 2 * HEAD_DIM
CONV_WIDTH = 31
CONV_STATE = CONV_WIDTH - 1
PAGE_SIZE = 128
PEER_HEADS = 8
N_KEYS = 128
PEER_TOPK = 16
PEER_HALF = 128
RMS_EPS = 1e-6
LN_EPS = 1e-5

VMEM_LIMIT_BYTES = 56 * 1024 * 1024

ROW_TILE = 512
ATT_TILE = 384
CONV_TILE = 384
CONV_HALO = 32
PEER_TOKENS = 256
PEER_ROWS = 16
PEER_CHUNK = PEER_ROWS * N_KEYS
SAMPLE_CONV_SEQS = 16

NEG_INF = float("-inf")


def _cparams(sem):
    return pltpu.CompilerParams(dimension_semantics=sem, vmem_limit_bytes=VMEM_LIMIT_BYTES)


def _rms(x, g):
    return x * lax.rsqrt(jnp.mean(x * x, axis=-1, keepdims=True) + RMS_EPS) * g


def _gelu(x):
    return 0.5 * x * (1.0 + lax.erf(x * math.sqrt(0.5)))


def _dot(a, b):
    return jnp.dot(a, b, preferred_element_type=F32)


def _dot_nt(a, b):
    return lax.dot_general(a, b, (((1,), (1,)), ((), ())), preferred_element_type=F32)


def _qkv_kernel(x_ref, g_ref, w_ref, q_ref, kb_ref, vb_ref, kf_ref, vf_ref):
    d = x_ref.shape[1]
    hb = _rms(x_ref[...], g_ref[...]).astype(BF16)
    q = _dot(hb, w_ref[:, 0:d])
    q_ref[...] = (q * (HEAD_DIM ** -0.5)).astype(BF16)
    k = _dot(hb, w_ref[:, d:2 * d])
    kf_ref[...] = k
    kb_ref[...] = k.astype(BF16)
    v = _dot(hb, w_ref[:, 2 * d:3 * d])
    vf_ref[...] = v
    vb_ref[...] = v.astype(BF16)


def _qkv(x, g, w):
    t, d = x.shape
    row = pl.BlockSpec((ROW_TILE, d), lambda i: (i, 0))
    return pl.pallas_call(
        _qkv_kernel,
        grid=(t // ROW_TILE,),
        in_specs=[row, pl.BlockSpec((1, d), lambda i: (0, 0)), pl.BlockSpec((d, 3 * d), lambda i: (0, 0))],
        out_specs=[row] * 5,
        out_shape=[jax.ShapeDtypeStruct((t, d), BF16)] * 3 + [jax.ShapeDtypeStruct((t, d), F32)] * 2,
        compiler_params=_cparams(("parallel",)),
        name="qkv",
    )(x, g.reshape(1, d), w)


def _lambda(lam_ref, lam_init):
    p = lam_ref[...]
    a = jnp.sum(p[0:1] * p[1:2], axis=-1, keepdims=True)
    b = jnp.sum(p[2:3] * p[3:4], axis=-1, keepdims=True)
    return jnp.exp(a) - jnp.exp(b) + lam_init


def _subln(o, g, lam_init):
    return _rms(o, g) * (1.0 - lam_init)


def _prompt_attn_kernel(q_ref, k_ref, v_ref, lam_ref, sg_ref, o_ref, *, lam_init):
    qi = pl.program_id(2)
    tq = q_ref.shape[0]
    q = q_ref[...]
    lane = lax.broadcasted_iota(jnp.int32, q.shape, 1)
    zero = jnp.zeros_like(q)
    qq = jnp.concatenate([jnp.where(lane < HEAD_DIM, q, zero), jnp.where(lane >= HEAD_DIM, q, zero)], axis=0)

    def step(j, carry, masked):
        m, l, acc = carry
        start = pl.multiple_of(j * tq, tq)
        k = k_ref[pl.ds(start, tq), :]
        v = v_ref[pl.ds(start, tq), :]
        s = _dot_nt(qq, k)
        if masked:
            r = lax.broadcasted_iota(jnp.int32, (tq, tq), 0)
            c = lax.broadcasted_iota(jnp.int32, (tq, tq), 1)
            keep = jnp.concatenate([c <= r, c <= r], axis=0)
            s = jnp.where(keep, s, NEG_INF)
        m_new = jnp.maximum(m, jnp.max(s, axis=-1, keepdims=True))
        alpha = jnp.exp(m - m_new)
        p = jnp.exp(s - m_new)
        l = alpha * l + jnp.sum(p, axis=-1, keepdims=True)
        acc = alpha * acc + _dot(p.astype(BF16), v)
        return m_new, l, acc

    init = (jnp.full((2 * tq, 1), NEG_INF, F32), jnp.zeros((2 * tq, 1), F32), jnp.zeros((2 * tq, V_HEAD_DIM), F32))
    carry = lax.fori_loop(0, qi, lambda j, c: step(j, c, False), init)
    m, l, acc = step(qi, carry, True)
    o = acc / l
    lam = _lambda(lam_ref, lam_init)
    o = o[:tq] - lam * o[tq:]
    o_ref[...] = _subln(o, sg_ref[...], lam_init).astype(o_ref.dtype)


def _prompt_attn(q, k, v, lam_params, subln_g, lam_init, n_batch, seq_pad):
    d = q.shape[1]
    nq = seq_pad // ATT_TILE
    qspec = pl.BlockSpec((ATT_TILE, V_HEAD_DIM), lambda b, h, i: (b * nq + i, h))
    kvspec = pl.BlockSpec((seq_pad, V_HEAD_DIM), lambda b, h, i: (b, h))
    return pl.pallas_call(
        functools.partial(_prompt_attn_kernel, lam_init=lam_init),
        grid=(n_batch, N_HEADS, nq),
        in_specs=[qspec, kvspec, kvspec,
                  pl.BlockSpec((4, HEAD_DIM), lambda b, h, i: (0, 0)),
                  pl.BlockSpec((1, V_HEAD_DIM), lambda b, h, i: (0, 0))],
        out_specs=qspec,
        out_shape=jax.ShapeDtypeStruct((n_batch * seq_pad, d), BF16),
        compiler_params=_cparams(("parallel", "parallel", "arbitrary")),
        name="prompt_attn",
    )(q, k, v, lam_params, subln_g.reshape(1, V_HEAD_DIM))


def _sample_attn_kernel(pt_ref, q_ref, kn_ref, vn_ref, k_ref, v_ref, ones_ref, lam_ref, sg_ref, o_ref,
                        m_scr, l_scr, acc_scr, *, lam_init):
    p = pl.program_id(1)
    q = q_ref[0]
    ones = ones_ref[...]

    @pl.when(p == 0)
    def _():
        s = _dot((kn_ref[0] * q).astype(BF16), ones)
        m_scr[...] = s
        l_scr[...] = jnp.ones_like(s)
        vn = vn_ref[0]
        acc_scr[...] = jnp.concatenate([vn, vn], axis=1)

    kp = k_ref[0]
    n_tok = kp.shape[0]
    prod = (kp * q[None]).reshape(n_tok * N_HEADS, V_HEAD_DIM).astype(BF16)
    s = _dot(prod, ones).reshape(n_tok, N_HEADS, 2 * V_HEAD_DIM)
    m_old = m_scr[...]
    m_new = jnp.maximum(m_old, jnp.max(s, axis=0))
    alpha = jnp.exp(m_old - m_new)
    pe = jnp.exp(s - m_new[None])
    l_scr[...] = alpha * l_scr[...] + jnp.sum(pe, axis=0)
    vp = v_ref[0]
    pv = jnp.concatenate([jnp.sum(pe[:, :, :V_HEAD_DIM] * vp, axis=0),
                          jnp.sum(pe[:, :, V_HEAD_DIM:] * vp, axis=0)], axis=1)
    acc_scr[...] = alpha * acc_scr[...] + pv
    m_scr[...] = m_new

    @pl.when(p == pl.num_programs(1) - 1)
    def _():
        o = acc_scr[...] / l_scr[...]
        lam = _lambda(lam_ref, lam_init)
        o = o[:, :V_HEAD_DIM] - lam * o[:, V_HEAD_DIM:]
        o_ref[0] = _subln(o, sg_ref[...], lam_init)


def _sample_attn(page_table, q, k_new, v_new, k_pool, v_pool, lam_params, subln_g, lam_init):
    n_seq, n_pages = page_table.shape
    hd = (N_HEADS, V_HEAD_DIM)
    d_idx = jnp.arange(V_HEAD_DIM)[:, None] >= HEAD_DIM
    c_idx = jnp.arange(2 * V_HEAD_DIM)[None, :] >= V_HEAD_DIM
    ones = (d_idx == c_idx).astype(BF16)
    tok = pl.BlockSpec((1,) + hd, lambda s, p, pt: (s, 0, 0))
    page = pl.BlockSpec((1, PAGE_SIZE) + hd, lambda s, p, pt: (pt[s, p], 0, 0, 0))
    grid_spec = pltpu.PrefetchScalarGridSpec(
        num_scalar_prefetch=1,
        grid=(n_seq, n_pages),
        in_specs=[tok, tok, tok, page, page,
                  pl.BlockSpec((V_HEAD_DIM, 2 * V_HEAD_DIM), lambda s, p, pt: (0, 0)),
                  pl.BlockSpec((4, HEAD_DIM), lambda s, p, pt: (0, 0)),
                  pl.BlockSpec((1, V_HEAD_DIM), lambda s, p, pt: (0, 0))],
        out_specs=tok,
        scratch_shapes=[pltpu.VMEM((N_HEADS, 2 * V_HEAD_DIM), F32)] * 3,
    )
    return pl.pallas_call(
        functools.partial(_sample_attn_kernel, lam_init=lam_init),
        grid_spec=grid_spec,
        out_shape=jax.ShapeDtypeStruct((n_seq,) + hd, F32),
        compiler_params=_cparams(("parallel", "arbitrary")),
        name="sample_attn",
    )(page_table, q, k_new, v_new, k_pool, v_pool, ones, lam_params, subln_g.reshape(1, V_HEAD_DIM))


def _oproj_kernel(o_ref, w_ref, x_ref, out_ref):
    out_ref[...] = x_ref[...] + _dot(o_ref[...], w_ref[...])


def _oproj(o, w, x):
    t, d = x.shape
    row = pl.BlockSpec((ROW_TILE, d), lambda i: (i, 0))
    return pl.pallas_call(
        _oproj_kernel,
        grid=(t // ROW_TILE,),
        in_specs=[row, pl.BlockSpec((d, d), lambda i: (0, 0)), row],
        out_specs=row,
        out_shape=jax.ShapeDtypeStruct((t, d), F32),
        compiler_params=_cparams(("parallel",)),
        name="oproj",
    )(o, w, x)


def _glu_kernel(x_ref, g_ref, w_ref, b_ref, u_ref):
    d = x_ref.shape[1]
    hb = _rms(x_ref[...], g_ref[...]).astype(BF16)
    a = _dot(hb, w_ref[:, 0:d]) + b_ref[:, 0:d]
    gate = _dot(hb, w_ref[:, d:2 * d]) + b_ref[:, d:2 * d]
    u_ref[...] = a * jax.nn.sigmoid(gate)


def _glu(x, g, w, b):
    t, d = x.shape
    row = pl.BlockSpec((ROW_TILE, d), lambda i: (i, 0))
    return pl.pallas_call(
        _glu_kernel,
        grid=(t // ROW_TILE,),
        in_specs=[row, pl.BlockSpec((1, d), lambda i: (0, 0)), pl.BlockSpec((d, 2 * d), lambda i: (0, 0)),
                  pl.BlockSpec((1, 2 * d), lambda i: (0, 0))],
        out_specs=row,
        out_shape=jax.ShapeDtypeStruct((t, d), F32),
        compiler_params=_cparams(("parallel",)),
        name="glu",
    )(x, g.reshape(1, d), w, b.reshape(1, 2 * d))


def _conv_tail(c, lng_ref, lnb_ref, w2_ref, x_ref, out_ref):
    mu = jnp.mean(c, axis=-1, keepdims=True)
    cc = c - mu
    var = jnp.mean(cc * cc, axis=-1, keepdims=True)
    cn = cc * lax.rsqrt(var + LN_EPS) * lng_ref[...] + lnb_ref[...]
    act = (cn * jax.nn.sigmoid(cn)).astype(BF16)
    out_ref[...] = x_ref[...] + _dot(act, w2_ref[...])


def _conv_prompt_kernel(u_ref, halo_ref, wdw_ref, bdw_ref, lng_ref, lnb_ref, w2_ref, x_ref, out_ref, buf,
                        *, tiles_per_seq):
    i = pl.program_id(0)
    tile = u_ref.shape[0]
    halo = halo_ref[...]
    buf[0:CONV_HALO, :] = jnp.where(i % tiles_per_seq == 0, jnp.zeros_like(halo), halo)
    buf[CONV_HALO:CONV_HALO + tile, :] = u_ref[...]
    off = CONV_HALO - CONV_STATE
    c = buf[off:off + tile, :] * wdw_ref[0:1, :]
    for w in range(1, CONV_WIDTH):
        c = c + buf[off + w:off + w + tile, :] * wdw_ref[w:w + 1, :]
    _conv_tail(c + bdw_ref[...], lng_ref, lnb_ref, w2_ref, x_ref, out_ref)


def _conv_prompt(u, x, w_dw, b_dw, ln_g, ln_b, w2, n_rows, seq_pad):
    d = u.shape[1]
    per_tile = CONV_TILE // CONV_HALO
    row = pl.BlockSpec((CONV_TILE, d), lambda i: (i, 0))
    vec = pl.BlockSpec((1, d), lambda i: (0, 0))
    return pl.pallas_call(
        functools.partial(_conv_prompt_kernel, tiles_per_seq=seq_pad // CONV_TILE),
        grid=(n_rows // CONV_TILE,),
        in_specs=[row, pl.BlockSpec((CONV_HALO, d), lambda i: (jnp.maximum(i * per_tile - 1, 0), 0)),
                  pl.BlockSpec((CONV_WIDTH, d), lambda i: (0, 0)), vec, vec, vec,
                  pl.BlockSpec((d, d), lambda i: (0, 0)), row],
        out_specs=row,
        out_shape=jax.ShapeDtypeStruct((n_rows, d), F32),
        scratch_shapes=[pltpu.VMEM((CONV_HALO + CONV_TILE, d), F32)],
        compiler_params=_cparams(("parallel",)),
        name="conv_prompt",
    )(u, u, w_dw, b_dw.reshape(1, d), ln_g.reshape(1, d), ln_b.reshape(1, d), w2, x)


def _conv_sample_kernel(st_ref, u_ref, wdw_ref, bdw_ref, lng_ref, lnb_ref, w2_ref, x_ref, out_ref):
    c = jnp.sum(st_ref[...] * wdw_ref[0:CONV_STATE, :][None], axis=1)
    c = c + u_ref[...] * wdw_ref[CONV_STATE:CONV_WIDTH, :] + bdw_ref[...]
    _conv_tail(c, lng_ref, lnb_ref, w2_ref, x_ref, out_ref)


def _conv_sample(state, u, x, w_dw, b_dw, ln_g, ln_b, w2, first_row):
    n_seq, _, d = state.shape
    ns = SAMPLE_CONV_SEQS
    base = first_row // ns
    row = pl.BlockSpec((ns, d), lambda i: (base + i, 0))
    vec = pl.BlockSpec((1, d), lambda i: (0, 0))
    return pl.pallas_call(
        _conv_sample_kernel,
        grid=(n_seq // ns,),
        in_specs=[pl.BlockSpec((ns, CONV_STATE, d), lambda i: (i, 0, 0)), row,
                  pl.BlockSpec((CONV_WIDTH, d), lambda i: (0, 0)), vec, vec, vec,
                  pl.BlockSpec((d, d), lambda i: (0, 0)), row],
        out_specs=pl.BlockSpec((ns, d), lambda i: (i, 0)),
        out_shape=jax.ShapeDtypeStruct((n_seq, d), F32),
        compiler_params=_cparams(("parallel",)),
        name="conv_sample",
    )(state, u, w_dw, b_dw.reshape(1, d), ln_g.reshape(1, d), ln_b.reshape(1, d), w2, x)


def _top_values(s, n):
    rows = []
    work = s
    for r in range(n):
        m = jnp.max(work, axis=0, keepdims=True)
        rows.append(m)
        if r + 1 < n:
            work = jnp.where(work == m, NEG_INF, work)
    return rows


def _peer_kernel(x_ref, g_ref, wq_ref, keys_ref, u_ref, vt_ref, o_ref,
                 h_scr, q_scr, s1_scr, s2_scr, e2_scr, c1_scr, tau_scr, ga_scr, w_scr, acc_scr):
    c = pl.program_id(1)

    @pl.when(c == 0)
    def _():
        hb = _rms(x_ref[...], g_ref[...]).astype(BF16)
        h_scr[...] = hb
        q_scr[...] = _dot_nt(wq_ref[...], hb)

        def head(hd, carry):
            base = pl.multiple_of(hd * (2 * PEER_HALF), 2 * PEER_HALF)
            q1 = q_scr[pl.ds(base, PEER_HALF), :].astype(BF16)
            q2 = q_scr[pl.ds(base + PEER_HALF, PEER_HALF), :].astype(BF16)
            s1 = _dot(keys_ref[hd, 0], q1)
            s2 = _dot(keys_ref[hd, 1], q2)
            t1 = _top_values(s1, PEER_TOPK)
            t2 = _top_values(s2, PEER_TOPK)
            sv2 = jnp.concatenate(t2, axis=0)
            cand = jnp.concatenate([t1[a] + sv2 for a in range(PEER_TOPK)], axis=0)
            tau = _top_values(cand, PEER_TOPK)[-1]
            top = t1[0] + t2[0]
            z = jnp.sum(jnp.where(cand >= tau, jnp.exp(cand - top), 0.0), axis=0, keepdims=True)
            s1_scr[hd] = s1
            s2_scr[hd] = s2
            c1_scr[hd] = jnp.exp(s1 - t1[0]) / z
            e2_scr[hd] = jnp.exp(s2 - t2[0])
            tau_scr[hd] = tau
            return carry

        lax.fori_loop(0, PEER_HEADS, head, 0)
        acc_scr[...] = jnp.zeros_like(acc_scr)

    a = _dot_nt(u_ref[...], h_scr[...])
    ga_scr[...] = _gelu(a)

    def key_row(il, carry):
        i1 = c * PEER_ROWS + il
        g = jnp.zeros((N_KEYS, x_ref.shape[0]), F32)
        for hd in range(PEER_HEADS):
            ssum = s1_scr[hd, pl.ds(i1, 1), :] + s2_scr[hd]
            g = g + jnp.where(ssum >= tau_scr[hd], e2_scr[hd], 0.0) * c1_scr[hd, pl.ds(i1, 1), :]
        r0 = pl.multiple_of(il * N_KEYS, N_KEYS)
        w_scr[pl.ds(r0, N_KEYS), :] = (g * ga_scr[pl.ds(r0, N_KEYS), :]).astype(BF16)
        return carry

    lax.fori_loop(0, PEER_ROWS, key_row, 0)
    acc_scr[...] += _dot(vt_ref[...], w_scr[...])

    @pl.when(c == pl.num_programs(1) - 1)
    def _():
        o_ref[...] = x_ref[...] + acc_scr[...].T


def _peer(x, g, wq_t, keys, u, v_t):
    t, d = x.shape
    tb = PEER_TOKENS
    n_exp = u.shape[0]
    row = pl.BlockSpec((tb, d), lambda i, c: (i, 0))
    return pl.pallas_call(
        _peer_kernel,
        grid=(t // tb, n_exp // PEER_CHUNK),
        in_specs=[row, pl.BlockSpec((1, d), lambda i, c: (0, 0)),
                  pl.BlockSpec(wq_t.shape, lambda i, c: (0, 0)),
                  pl.BlockSpec(keys.shape, lambda i, c: (0, 0, 0, 0)),
                  pl.BlockSpec((PEER_CHUNK, d), lambda i, c: (c, 0)),
                  pl.BlockSpec((d, PEER_CHUNK), lambda i, c: (0, c))],
        out_specs=row,
        out_shape=jax.ShapeDtypeStruct((t, d), F32),
        scratch_shapes=[pltpu.VMEM((tb, d), BF16),
                        pltpu.VMEM((PEER_HEADS * 2 * PEER_HALF, tb), F32),
                        pltpu.VMEM((PEER_HEADS, N_KEYS, tb), F32),
                        pltpu.VMEM((PEER_HEADS, N_KEYS, tb), F32),
                        pltpu.VMEM((PEER_HEADS, N_KEYS, tb), F32),
                        pltpu.VMEM((PEER_HEADS, N_KEYS, tb), F32),
                        pltpu.VMEM((PEER_HEADS, 1, tb), F32),
                        pltpu.VMEM((PEER_CHUNK, tb), F32),
                        pltpu.VMEM((PEER_CHUNK, tb), BF16),
                        pltpu.VMEM((d, tb), F32)],
        compiler_params=_cparams(("parallel", "arbitrary")),
        name="peer",
    )(x, g.reshape(1, d), wq_t, keys, u, v_t)


def _final_norm_kernel(x_ref, g_ref, y_ref):
    y_ref[...] = _rms(x_ref[...], g_ref[...])


def _final_norm(x, g):
    t, d = x.shape
    row = pl.BlockSpec((ROW_TILE, d), lambda i: (i, 0))
    return pl.pallas_call(
        _final_norm_kernel,
        grid=(t // ROW_TILE,),
        in_specs=[row, pl.BlockSpec((1, d), lambda i: (0, 0))],
        out_specs=row,
        out_shape=jax.ShapeDtypeStruct((t, d), F32),
        compiler_params=_cparams(("parallel",)),
        name="final_norm",
    )(x, g.reshape(1, d))


def kernel(x_prompt, x_sample, cache_k, cache_v, state_conv, page_table, meta_tokens, g_mix, g_ffn, g_final, w_qkv, lambda_q1, lambda_k1, lambda_q2, lambda_k2, subln_g, w_o_attn, w_pw1, b_pw1, w_dw, b_dw, ln_g, ln_b, w_pw2, peer_w_query, peer_sub_keys, peer_u, peer_v):
    n_batch, seq, d = x_prompt.shape
    n_dec = x_sample.shape[0]
    depth = g_mix.shape[0]
    seq_len = seq + N_META
    seq_pad = -(-seq_len // ATT_TILE) * ATT_TILE
    n_prompt_rows = n_batch * seq_pad
    assert seq_pad % CONV_TILE == 0 and n_prompt_rows % SAMPLE_CONV_SEQS == 0
    assert x_sample.shape[1] == 1 and n_dec % SAMPLE_CONV_SEQS == 0
    t_total = -(-(n_prompt_rows + n_dec) // ROW_TILE) * ROW_TILE
    assert t_total % PEER_TOKENS == 0
    n_tail = t_total - n_prompt_rows - n_dec

    def with_sample_rows(prompt_rows, sample_rows):
        return jnp.concatenate([prompt_rows, sample_rows, jnp.zeros((n_tail, d), prompt_rows.dtype)], axis=0)

    meta = jnp.broadcast_to(meta_tokens.astype(F32)[None], (n_batch, N_META, d))
    xp = jnp.concatenate([meta, x_prompt, jnp.zeros((n_batch, seq_pad - seq_len, d), F32)], axis=1)
    x = with_sample_rows(xp.reshape(n_prompt_rows, d), x_sample.reshape(n_dec, d))

    def prompt_part(a):
        return a[:n_prompt_rows].reshape(n_batch, seq_pad, -1)[:, :seq_len]

    def sample_part(a):
        return a[n_prompt_rows:n_prompt_rows + n_dec]

    kp_l, vp_l, cp_l, ks_l, vs_l, cs_l = [], [], [], [], [], []
    for i in range(depth):
        j = i // 2
        if i % 2 == 0:
            lam_init = 0.8 - 0.6 * math.exp(-0.3 * i)
            lam_params = jnp.stack([lambda_q1[j], lambda_k1[j], lambda_q2[j], lambda_k2[j]])
            qs, kb, vb, kf, vf = _qkv(x, g_mix[i], w_qkv[j].astype(BF16))
            o_p = _prompt_attn(qs, kb, vb, lam_params, subln_g[j], lam_init, n_batch, seq_pad)
            heads = (n_dec, N_HEADS, V_HEAD_DIM)
            o_s = _sample_attn(page_table, sample_part(qs).astype(F32).reshape(heads),
                               sample_part(kf).reshape(heads), sample_part(vf).reshape(heads),
                               cache_k[j], cache_v[j], lam_params, subln_g[j], lam_init)
            o = with_sample_rows(o_p, o_s.reshape(n_dec, d).astype(BF16))
            x = _oproj(o, w_o_attn[j].astype(BF16), x)
            kv_shape = (n_batch, seq_len, N_HEADS, V_HEAD_DIM)
            kp_l.append(prompt_part(kf).reshape(kv_shape))
            vp_l.append(prompt_part(vf).reshape(kv_shape))
            ks_l.append(sample_part(kf).reshape(n_dec, 1, N_HEADS, V_HEAD_DIM))
            vs_l.append(sample_part(vf).reshape(n_dec, 1, N_HEADS, V_HEAD_DIM))
        else:
            u = _glu(x, g_mix[i], w_pw1[j].astype(BF16), b_pw1[j])
            w2 = w_pw2[j].astype(BF16)
            x_p = _conv_prompt(u, x, w_dw[j], b_dw[j], ln_g[j], ln_b[j], w2, n_prompt_rows, seq_pad)
            x_s = _conv_sample(state_conv[j], u, x, w_dw[j], b_dw[j], ln_g[j], ln_b[j], w2, n_prompt_rows)
            x = with_sample_rows(x_p, x_s)
            cp_l.append(prompt_part(u)[:, seq_len - CONV_STATE:])
            cs_l.append(jnp.concatenate([state_conv[j][:, 1:], sample_part(u)[:, None]], axis=1))
        x = _peer(x, g_ffn[i], peer_w_query[i].T.astype(BF16), peer_sub_keys[i].astype(BF16),
                  peer_u[i].astype(BF16), peer_v[i].T.astype(BF16))
    y = _final_norm(x, g_final)
    return (prompt_part(y)[:, N_META:], sample_part(y).reshape(n_dec, 1, d),
            jnp.stack(kp_l), jnp.stack(vp_l), jnp.stack(cp_l),
            jnp.stack(ks_l), jnp.stack(vs_l), jnp.stack(cs_l))
```

```python
import functools
import math

import jax
import jax.numpy as jnp
from jax import lax
from jax.experimental import pallas as pl
from jax.experimental.pallas import tpu as pltpu

F32 = jnp.float32
BF16 = jnp.bfloat16

D_MODEL = 1024
N_META = 16
N_HEADS = 8
HEAD_DIM = 64
V_HEAD_DIM = 2 * HEAD_DIM
CONV_WIDTH = 31
CONV_STATE = CONV_WIDTH - 1
PAGE_SIZE = 128
PEER_HEADS = 8
N_KEYS = 128
PEER_TOPK = 16
PEER_HALF = 128
RMS_EPS = 1e-6
LN_EPS = 1e-5

VMEM_LIMIT_BYTES = 56 * 1024 * 1024

ROW_TILE = 512
ATT_TILE = 384
CONV_TILE = 384
CONV_HALO = 32
PEER_TOKENS = 256
PEER_ROWS = 16
PEER_CHUNK = PEER_ROWS * N_KEYS
PEER_SUB = 2 * N_KEYS
PEER_LANES = 128
SAMPLE_CONV_SEQS = 16
SAMPLE_PAGES = 4

NEG_INF = float("-inf")


def _cparams(sem):
    return pltpu.CompilerParams(dimension_semantics=sem, vmem_limit_bytes=VMEM_LIMIT_BYTES)


def _rms(x, g):
    return x * lax.rsqrt(jnp.mean(x * x, axis=-1, keepdims=True) + RMS_EPS) * g


def _gelu(x):
    return 0.5 * x * (1.0 + lax.erf(x * math.sqrt(0.5)))


def _dot(a, b):
    return jnp.dot(a, b, preferred_element_type=F32)


def _dot_nt(a, b):
    return lax.dot_general(a, b, (((1,), (1,)), ((), ())), preferred_element_type=F32)


def _qkv_kernel(x_ref, g_ref, w_ref, q_ref, kb_ref, vb_ref, kf_ref, vf_ref):
    d = x_ref.shape[1]
    hb = _rms(x_ref[...], g_ref[...]).astype(BF16)
    q = _dot(hb, w_ref[:, 0:d])
    q_ref[...] = (q * (HEAD_DIM ** -0.5)).astype(BF16)
    k = _dot(hb, w_ref[:, d:2 * d])
    kf_ref[...] = k
    kb_ref[...] = k.astype(BF16)
    v = _dot(hb, w_ref[:, 2 * d:3 * d])
    vf_ref[...] = v
    vb_ref[...] = v.astype(BF16)


def _qkv(x, g, w):
    t, d = x.shape
    row = pl.BlockSpec((ROW_TILE, d), lambda i: (i, 0))
    return pl.pallas_call(
        _qkv_kernel,
        grid=(t // ROW_TILE,),
        in_specs=[row, pl.BlockSpec((1, d), lambda i: (0, 0)), pl.BlockSpec((d, 3 * d), lambda i: (0, 0))],
        out_specs=[row] * 5,
        out_shape=[jax.ShapeDtypeStruct((t, d), BF16)] * 3 + [jax.ShapeDtypeStruct((t, d), F32)] * 2,
        compiler_params=_cparams(("parallel",)),
        name="qkv",
    )(x, g.reshape(1, d), w)


def _lambda(lam_ref, lam_init):
    p = lam_ref[...]
    a = jnp.sum(p[0:1] * p[1:2], axis=-1, keepdims=True)
    b = jnp.sum(p[2:3] * p[3:4], axis=-1, keepdims=True)
    return jnp.exp(a) - jnp.exp(b) + lam_init


def _subln(o, g, lam_init):
    return _rms(o, g) * (1.0 - lam_init)


def _prompt_attn_kernel(q_ref, k_ref, v_ref, lam_ref, sg_ref, o_ref, *, lam_init):
    qi = pl.program_id(2)
    tq = q_ref.shape[0]
    q = q_ref[...]
    lane = lax.broadcasted_iota(jnp.int32, q.shape, 1)
    zero = jnp.zeros_like(q)
    qq = jnp.concatenate([jnp.where(lane < HEAD_DIM, q, zero), jnp.where(lane >= HEAD_DIM, q, zero)], axis=0)

    def step(j, carry, masked):
        m, l, acc = carry
        start = pl.multiple_of(j * tq, tq)
        k = k_ref[pl.ds(start, tq), :]
        v = v_ref[pl.ds(start, tq), :]
        s = _dot_nt(qq, k)
        if masked:
            r = lax.broadcasted_iota(jnp.int32, (tq, tq), 0)
            c = lax.broadcasted_iota(jnp.int32, (tq, tq), 1)
            keep = jnp.concatenate([c <= r, c <= r], axis=0)
            s = jnp.where(keep, s, NEG_INF)
        m_new = jnp.maximum(m, jnp.max(s, axis=-1, keepdims=True))
        alpha = jnp.exp(m - m_new)
        p = jnp.exp(s - m_new)
        l = alpha * l + jnp.sum(p, axis=-1, keepdims=True)
        acc = alpha * acc + _dot(p.astype(BF16), v)
        return m_new, l, acc

    init = (jnp.full((2 * tq, 1), NEG_INF, F32), jnp.zeros((2 * tq, 1), F32), jnp.zeros((2 * tq, V_HEAD_DIM), F32))
    carry = lax.fori_loop(0, qi, lambda j, c: step(j, c, False), init)
    m, l, acc = step(qi, carry, True)
    o = acc / l
    lam = _lambda(lam_ref, lam_init)
    o = o[:tq] - lam * o[tq:]
    o_ref[...] = _subln(o, sg_ref[...], lam_init).astype(o_ref.dtype)


def _prompt_attn(q, k, v, lam_params, subln_g, lam_init, n_batch, seq_pad):
    d = q.shape[1]
    nq = seq_pad // ATT_TILE
    qspec = pl.BlockSpec((ATT_TILE, V_HEAD_DIM), lambda b, h, i: (b * nq + i, h))
    kvspec = pl.BlockSpec((seq_pad, V_HEAD_DIM), lambda b, h, i: (b, h))
    return pl.pallas_call(
        functools.partial(_prompt_attn_kernel, lam_init=lam_init),
        grid=(n_batch, N_HEADS, nq),
        in_specs=[qspec, kvspec, kvspec,
                  pl.BlockSpec((4, HEAD_DIM), lambda b, h, i: (0, 0)),
                  pl.BlockSpec((1, V_HEAD_DIM), lambda b, h, i: (0, 0))],
        out_specs=qspec,
        out_shape=jax.ShapeDtypeStruct((n_batch * seq_pad, d), BF16),
        compiler_params=_cparams(("parallel", "parallel", "arbitrary")),
        name="prompt_attn",
    )(q, k, v, lam_params, subln_g.reshape(1, V_HEAD_DIM))


def _sample_attn_kernel(pt_ref, q_ref, kn_ref, vn_ref, *refs, lam_init):
    k_refs = refs[:SAMPLE_PAGES]
    v_refs = refs[SAMPLE_PAGES:2 * SAMPLE_PAGES]
    ones_ref, lam_ref, sg_ref, o_ref, m_scr, l_scr, acc_scr = refs[2 * SAMPLE_PAGES:]
    p = pl.program_id(1)
    q = q_ref[0]
    ones = ones_ref[...]

    @pl.when(p == 0)
    def _():
        s = _dot((kn_ref[0] * q).astype(BF16), ones)
        m_scr[...] = s
        l_scr[...] = jnp.ones_like(s)
        vn = vn_ref[0]
        acc_scr[...] = jnp.concatenate([vn, vn], axis=1)

    scores = []
    for k_ref in k_refs:
        kp = k_ref[...]
        n_tok = kp.shape[0]
        prod = (kp * q[None]).reshape(n_tok * N_HEADS, V_HEAD_DIM).astype(BF16)
        scores.append(_dot(prod, ones).reshape(n_tok, N_HEADS, 2 * V_HEAD_DIM))
    m_old = m_scr[...]
    m_new = m_old
    for s in scores:
        m_new = jnp.maximum(m_new, jnp.max(s, axis=0))
    alpha = jnp.exp(m_old - m_new)
    l = alpha * l_scr[...]
    acc = alpha * acc_scr[...]
    for s, v_ref in zip(scores, v_refs):
        pe = jnp.exp(s - m_new[None])
        l = l + jnp.sum(pe, axis=0)
        vp = v_ref[...]
        acc = acc + jnp.concatenate([jnp.sum(pe[:, :, :V_HEAD_DIM] * vp, axis=0),
                                     jnp.sum(pe[:, :, V_HEAD_DIM:] * vp, axis=0)], axis=1)
    l_scr[...] = l
    acc_scr[...] = acc
    m_scr[...] = m_new

    @pl.when(p == pl.num_programs(1) - 1)
    def _():
        o = acc_scr[...] / l_scr[...]
        lam = _lambda(lam_ref, lam_init)
        o = o[:, :V_HEAD_DIM] - lam * o[:, V_HEAD_DIM:]
        o_ref[0] = _subln(o, sg_ref[...], lam_init)


def _sample_attn(page_table, q, k_new, v_new, k_pool, v_pool, layer, lam_params, subln_g, lam_init):
    n_seq, n_pages = page_table.shape
    assert n_pages % SAMPLE_PAGES == 0
    hd = (N_HEADS, V_HEAD_DIM)
    d_idx = jnp.arange(V_HEAD_DIM)[:, None] >= HEAD_DIM
    c_idx = jnp.arange(2 * V_HEAD_DIM)[None, :] >= V_HEAD_DIM
    ones = (d_idx == c_idx).astype(BF16)
    tok = pl.BlockSpec((1,) + hd, lambda s, p, pt: (s, 0, 0))

    def page(k):
        return pl.BlockSpec((None, None, PAGE_SIZE) + hd,
                            lambda s, p, pt: (layer, pt[s, p * SAMPLE_PAGES + k], 0, 0, 0))

    pages = [page(k) for k in range(SAMPLE_PAGES)]
    grid_spec = pltpu.PrefetchScalarGridSpec(
        num_scalar_prefetch=1,
        grid=(n_seq, n_pages // SAMPLE_PAGES),
        in_specs=[tok, tok, tok] + pages + pages +
                 [pl.BlockSpec((V_HEAD_DIM, 2 * V_HEAD_DIM), lambda s, p, pt: (0, 0)),
                  pl.BlockSpec((4, HEAD_DIM), lambda s, p, pt: (0, 0)),
                  pl.BlockSpec((1, V_HEAD_DIM), lambda s, p, pt: (0, 0))],
        out_specs=tok,
        scratch_shapes=[pltpu.VMEM((N_HEADS, 2 * V_HEAD_DIM), F32)] * 3,
    )
    return pl.pallas_call(
        functools.partial(_sample_attn_kernel, lam_init=lam_init),
        grid_spec=grid_spec,
        out_shape=jax.ShapeDtypeStruct((n_seq,) + hd, F32),
        compiler_params=_cparams(("parallel", "arbitrary")),
        name="sample_attn",
    )(page_table, q, k_new, v_new, *([k_pool] * SAMPLE_PAGES), *([v_pool] * SAMPLE_PAGES),
      ones, lam_params, subln_g.reshape(1, V_HEAD_DIM))


def _oproj_kernel(o_ref, w_ref, x_ref, out_ref):
    out_ref[...] = x_ref[...] + _dot(o_ref[...], w_ref[...])


def _oproj(o, w, x):
    t, d = x.shape
    row = pl.BlockSpec((ROW_TILE, d), lambda i: (i, 0))
    return pl.pallas_call(
        _oproj_kernel,
        grid=(t // ROW_TILE,),
        in_specs=[row, pl.BlockSpec((d, d), lambda i: (0, 0)), row],
        out_specs=row,
        out_shape=jax.ShapeDtypeStruct((t, d), F32),
        compiler_params=_cparams(("parallel",)),
        name="oproj",
    )(o, w, x)


def _glu_kernel(x_ref, g_ref, w_ref, b_ref, u_ref):
    d = x_ref.shape[1]
    hb = _rms(x_ref[...], g_ref[...]).astype(BF16)
    a = _dot(hb, w_ref[:, 0:d]) + b_ref[:, 0:d]
    gate = _dot(hb, w_ref[:, d:2 * d]) + b_ref[:, d:2 * d]
    u_ref[...] = a * jax.nn.sigmoid(gate)


def _glu(x, g, w, b):
    t, d = x.shape
    row = pl.BlockSpec((ROW_TILE, d), lambda i: (i, 0))
    return pl.pallas_call(
        _glu_kernel,
        grid=(t // ROW_TILE,),
        in_specs=[row, pl.BlockSpec((1, d), lambda i: (0, 0)), pl.BlockSpec((d, 2 * d), lambda i: (0, 0)),
                  pl.BlockSpec((1, 2 * d), lambda i: (0, 0))],
        out_specs=row,
        out_shape=jax.ShapeDtypeStruct((t, d), F32),
        compiler_params=_cparams(("parallel",)),
        name="glu",
    )(x, g.reshape(1, d), w, b.reshape(1, 2 * d))


def _conv_tail(c, lng_ref, lnb_ref, w2_ref, x_ref, out_ref):
    mu = jnp.mean(c, axis=-1, keepdims=True)
    cc = c - mu
    var = jnp.mean(cc * cc, axis=-1, keepdims=True)
    cn = cc * lax.rsqrt(var + LN_EPS) * lng_ref[...] + lnb_ref[...]
    act = (cn * jax.nn.sigmoid(cn)).astype(BF16)
    out_ref[...] = x_ref[...] + _dot(act, w2_ref[...])


def _conv_prompt_kernel(u_ref, halo_ref, wdw_ref, bdw_ref, lng_ref, lnb_ref, w2_ref, x_ref, out_ref, buf,
                        *, tiles_per_seq):
    i = pl.program_id(0)
    tile = u_ref.shape[0]
    halo = halo_ref[...]
    buf[0:CONV_HALO, :] = jnp.where(i % tiles_per_seq == 0, jnp.zeros_like(halo), halo)
    buf[CONV_HALO:CONV_HALO + tile, :] = u_ref[...]
    off = CONV_HALO - CONV_STATE
    c = buf[off:off + tile, :] * wdw_ref[0:1, :]
    for w in range(1, CONV_WIDTH):
        c = c + buf[off + w:off + w + tile, :] * wdw_ref[w:w + 1, :]
    _conv_tail(c + bdw_ref[...], lng_ref, lnb_ref, w2_ref, x_ref, out_ref)


def _conv_prompt(u, x, w_dw, b_dw, ln_g, ln_b, w2, n_rows, seq_pad):
    d = u.shape[1]
    per_tile = CONV_TILE // CONV_HALO
    row = pl.BlockSpec((CONV_TILE, d), lambda i: (i, 0))
    vec = pl.BlockSpec((1, d), lambda i: (0, 0))
    return pl.pallas_call(
        functools.partial(_conv_prompt_kernel, tiles_per_seq=seq_pad // CONV_TILE),
        grid=(n_rows // CONV_TILE,),
        in_specs=[row, pl.BlockSpec((CONV_HALO, d), lambda i: (jnp.maximum(i * per_tile - 1, 0), 0)),
                  pl.BlockSpec((CONV_WIDTH, d), lambda i: (0, 0)), vec, vec, vec,
                  pl.BlockSpec((d, d), lambda i: (0, 0)), row],
        out_specs=row,
        out_shape=jax.ShapeDtypeStruct((n_rows, d), F32),
        scratch_shapes=[pltpu.VMEM((CONV_HALO + CONV_TILE, d), F32)],
        compiler_params=_cparams(("parallel",)),
        name="conv_prompt",
    )(u, u, w_dw, b_dw.reshape(1, d), ln_g.reshape(1, d), ln_b.reshape(1, d), w2, x)


def _conv_sample_kernel(st_ref, u_ref, wdw_ref, bdw_ref, lng_ref, lnb_ref, w2_ref, x_ref, out_ref):
    c = jnp.sum(st_ref[...] * wdw_ref[0:CONV_STATE, :][None], axis=1)
    c = c + u_ref[...] * wdw_ref[CONV_STATE:CONV_WIDTH, :] + bdw_ref[...]
    _conv_tail(c, lng_ref, lnb_ref, w2_ref, x_ref, out_ref)


def _conv_sample(state, u, x, w_dw, b_dw, ln_g, ln_b, w2, first_row):
    n_seq, _, d = state.shape
    ns = SAMPLE_CONV_SEQS
    base = first_row // ns
    row = pl.BlockSpec((ns, d), lambda i: (base + i, 0))
    vec = pl.BlockSpec((1, d), lambda i: (0, 0))
    return pl.pallas_call(
        _conv_sample_kernel,
        grid=(n_seq // ns,),
        in_specs=[pl.BlockSpec((ns, CONV_STATE, d), lambda i: (i, 0, 0)), row,
                  pl.BlockSpec((CONV_WIDTH, d), lambda i: (0, 0)), vec, vec, vec,
                  pl.BlockSpec((d, d), lambda i: (0, 0)), row],
        out_specs=pl.BlockSpec((ns, d), lambda i: (i, 0)),
        out_shape=jax.ShapeDtypeStruct((n_seq, d), F32),
        compiler_params=_cparams(("parallel",)),
        name="conv_sample",
    )(state, u, w_dw, b_dw.reshape(1, d), ln_g.reshape(1, d), ln_b.reshape(1, d), w2, x)


def _top_values(s, n):
    rows = []
    work = s
    for r in range(n):
        m = jnp.max(work, axis=0, keepdims=True)
        rows.append(m)
        if r + 1 < n:
            work = jnp.where(work == m, NEG_INF, work)
    return rows


def _pair_candidates(t1, t2):
    rows = [t1[a] + t2[b] for a in range(PEER_TOPK) for b in range(PEER_TOPK // (a + 1))]
    pad = -len(rows) % 8
    rows += [jnp.full_like(rows[0], NEG_INF)] * pad
    return jnp.concatenate(rows, axis=0)


def _peer_kernel(x_ref, g_ref, wq_ref, keys_ref, u_ref, vt_ref, o_ref,
                 h_scr, q_scr, s1_scr, s2_scr, e2_scr, c1_scr, tau_scr, acc_scr):
    c = pl.program_id(1)
    tb = x_ref.shape[0]

    @pl.when(c == 0)
    def _():
        hb = _rms(x_ref[...], g_ref[...]).astype(BF16)
        h_scr[...] = hb
        q_scr[...] = _dot_nt(wq_ref[...], hb)

        def head(hd, carry):
            base = pl.multiple_of(hd * (2 * PEER_HALF), 2 * PEER_HALF)
            q1 = q_scr[pl.ds(base, PEER_HALF), :].astype(BF16)
            q2 = q_scr[pl.ds(base + PEER_HALF, PEER_HALF), :].astype(BF16)
            s1 = _dot(keys_ref[hd, 0], q1)
            s2 = _dot(keys_ref[hd, 1], q2)
            t1 = _top_values(s1, PEER_TOPK)
            t2 = _top_values(s2, PEER_TOPK)
            cand = _pair_candidates(t1, t2)
            tau = _top_values(cand, PEER_TOPK)[-1]
            top = t1[0] + t2[0]
            z = jnp.sum(jnp.where(cand >= tau, jnp.exp(cand - top), 0.0), axis=0, keepdims=True)
            s1_scr[hd] = s1
            s2_scr[hd] = s2
            c1_scr[hd] = jnp.exp(s1 - t1[0]) * (1.0 / z)
            e2_scr[hd] = jnp.exp(s2 - t2[0])
            tau_scr[hd] = tau
            return carry

        lax.fori_loop(0, PEER_HEADS, head, 0)
        acc_scr[...] = jnp.zeros_like(acc_scr)

    hb = h_scr[...]
    rows_per_sub = PEER_SUB // N_KEYS
    for r in range(PEER_CHUNK // PEER_SUB):
        ga = _gelu(_dot_nt(u_ref[r * PEER_SUB:(r + 1) * PEER_SUB, :], hb))
        w_rows = []
        for kr in range(rows_per_sub):
            i1 = c * PEER_ROWS + r * rows_per_sub + kr
            s1_rows = [s1_scr[hd, pl.ds(i1, 1), :] for hd in range(PEER_HEADS)]
            c1_rows = [c1_scr[hd, pl.ds(i1, 1), :] for hd in range(PEER_HEADS)]
            cols = []
            for l0 in range(0, tb, PEER_LANES):
                ls = slice(l0, l0 + PEER_LANES)
                g = None
                for hd in range(PEER_HEADS):
                    ssum = s1_rows[hd][:, ls] + s2_scr[hd, :, ls]
                    term = jnp.where(ssum >= tau_scr[hd, :, ls], e2_scr[hd, :, ls], 0.0) * c1_rows[hd][:, ls]
                    g = term if g is None else g + term
                cols.append((g * ga[kr * N_KEYS:(kr + 1) * N_KEYS, ls]).astype(BF16))
            w_rows.append(jnp.concatenate(cols, axis=1))
        w = jnp.concatenate(w_rows, axis=0)
        acc_scr[...] += _dot(vt_ref[:, r * PEER_SUB:(r + 1) * PEER_SUB], w)

    @pl.when(c == pl.num_programs(1) - 1)
    def _():
        o_ref[...] = x_ref[...] + acc_scr[...].T


def _peer(x, g, wq_t, keys, u, v_t):
    t, d = x.shape
    tb = PEER_TOKENS
    n_exp = u.shape[0]
    row = pl.BlockSpec((tb, d), lambda i, c: (i, 0))
    return pl.pallas_call(
        _peer_kernel,
        grid=(t // tb, n_exp // PEER_CHUNK),
        in_specs=[row, pl.BlockSpec((1, d), lambda i, c: (0, 0)),
                  pl.BlockSpec(wq_t.shape, lambda i, c: (0, 0)),
                  pl.BlockSpec(keys.shape, lambda i, c: (0, 0, 0, 0)),
                  pl.BlockSpec((PEER_CHUNK, d), lambda i, c: (c, 0)),
                  pl.BlockSpec((d, PEER_CHUNK), lambda i, c: (0, c))],
        out_specs=row,
        out_shape=jax.ShapeDtypeStruct((t, d), F32),
        scratch_shapes=[pltpu.VMEM((tb, d), BF16),
                        pltpu.VMEM((PEER_HEADS * 2 * PEER_HALF, tb), F32),
                        pltpu.VMEM((PEER_HEADS, N_KEYS, tb), F32),
                        pltpu.VMEM((PEER_HEADS, N_KEYS, tb), F32),
                        pltpu.VMEM((PEER_HEADS, N_KEYS, tb), F32),
                        pltpu.VMEM((PEER_HEADS, N_KEYS, tb), F32),
                        pltpu.VMEM((PEER_HEADS, 1, tb), F32),
                        pltpu.VMEM((d, tb), F32)],
        compiler_params=_cparams(("parallel", "arbitrary")),
        name="peer",
    )(x, g.reshape(1, d), wq_t, keys, u, v_t)


def _final_norm_kernel(x_ref, g_ref, y_ref):
    y_ref[...] = _rms(x_ref[...], g_ref[...])


def _final_norm(x, g):
    t, d = x.shape
    row = pl.BlockSpec((ROW_TILE, d), lambda i: (i, 0))
    return pl.pallas_call(
        _final_norm_kernel,
        grid=(t // ROW_TILE,),
        in_specs=[row, pl.BlockSpec((1, d), lambda i: (0, 0))],
        out_specs=row,
        out_shape=jax.ShapeDtypeStruct((t, d), F32),
        compiler_params=_cparams(("parallel",)),
        name="final_norm",
    )(x, g.reshape(1, d))


def kernel(x_prompt, x_sample, cache_k, cache_v, state_conv, page_table, meta_tokens, g_mix, g_ffn, g_final, w_qkv, lambda_q1, lambda_k1, lambda_q2, lambda_k2, subln_g, w_o_attn, w_pw1, b_pw1, w_dw, b_dw, ln_g, ln_b, w_pw2, peer_w_query, peer_sub_keys, peer_u, peer_v):
    n_batch, seq, d = x_prompt.shape
    n_dec = x_sample.shape[0]
    depth = g_mix.shape[0]
    seq_len = seq + N_META
    seq_pad = -(-seq_len // ATT_TILE) * ATT_TILE
    n_prompt_rows = n_batch * seq_pad
    assert seq_pad % CONV_TILE == 0 and n_prompt_rows % SAMPLE_CONV_SEQS == 0
    assert x_sample.shape[1] == 1 and n_dec % SAMPLE_CONV_SEQS == 0
    t_total = -(-(n_prompt_rows + n_dec) // ROW_TILE) * ROW_TILE
    assert t_total % PEER_TOKENS == 0
    n_tail = t_total - n_prompt_rows - n_dec

    def with_sample_rows(prompt_rows, sample_rows):
        return jnp.concatenate([prompt_rows, sample_rows, jnp.zeros((n_tail, d), prompt_rows.dtype)], axis=0)

    meta = jnp.broadcast_to(meta_tokens.astype(F32)[None], (n_batch, N_META, d))
    xp = jnp.concatenate([meta, x_prompt, jnp.zeros((n_batch, seq_pad - seq_len, d), F32)], axis=1)
    x = with_sample_rows(xp.reshape(n_prompt_rows, d), x_sample.reshape(n_dec, d))

    def prompt_part(a):
        return a[:n_prompt_rows].reshape(n_batch, seq_pad, -1)[:, :seq_len]

    def sample_part(a):
        return a[n_prompt_rows:n_prompt_rows + n_dec]

    kp_l, vp_l, cp_l, ks_l, vs_l, cs_l = [], [], [], [], [], []
    for i in range(depth):
        j = i // 2
        if i % 2 == 0:
            lam_init = 0.8 - 0.6 * math.exp(-0.3 * i)
            lam_params = jnp.stack([lambda_q1[j], lambda_k1[j], lambda_q2[j], lambda_k2[j]])
            qs, kb, vb, kf, vf = _qkv(x, g_mix[i], w_qkv[j].astype(BF16))
            o_p = _prompt_attn(qs, kb, vb, lam_params, subln_g[j], lam_init, n_batch, seq_pad)
            heads = (n_dec, N_HEADS, V_HEAD_DIM)
            o_s = _sample_attn(page_table, sample_part(qs).astype(F32).reshape(heads),
                               sample_part(kf).reshape(heads), sample_part(vf).reshape(heads),
                               cache_k, cache_v, j, lam_params, subln_g[j], lam_init)
            o = with_sample_rows(o_p, o_s.reshape(n_dec, d).astype(BF16))
            x = _oproj(o, w_o_attn[j].astype(BF16), x)
            kv_shape = (n_batch, seq_len, N_HEADS, V_HEAD_DIM)
            kp_l.append(prompt_part(kf).reshape(kv_shape))
            vp_l.append(prompt_part(vf).reshape(kv_shape))
            ks_l.append(sample_part(kf).reshape(n_dec, 1, N_HEADS, V_HEAD_DIM))
            vs_l.append(sample_part(vf).reshape(n_dec, 1, N_HEADS, V_HEAD_DIM))
        else:
            u = _glu(x, g_mix[i], w_pw1[j].astype(BF16), b_pw1[j])
            w2 = w_pw2[j].astype(BF16)
            x_p = _conv_prompt(u, x, w_dw[j], b_dw[j], ln_g[j], ln_b[j], w2, n_prompt_rows, seq_pad)
            x_s = _conv_sample(state_conv[j], u, x, w_dw[j], b_dw[j], ln_g[j], ln_b[j], w2, n_prompt_rows)
            x = with_sample_rows(x_p, x_s)
            cp_l.append(prompt_part(u)[:, seq_len - CONV_STATE:])
            cs_l.append(jnp.concatenate([state_conv[j][:, 1:], sample_part(u)[:, None]], axis=1))
        x = _peer(x, g_ffn[i], peer_w_query[i].T.astype(BF16), peer_sub_keys[i].astype(BF16),
                  peer_u[i].astype(BF16), peer_v[i].T.astype(BF16))
    y = _final_norm(x, g_final)
    return (prompt_part(y)[:, N_META:], sample_part(y).reshape(n_dec, 1, d),
            jnp.stack(kp_l), jnp.stack(vp_l), jnp.stack(cp_l),
            jnp.stack(ks_l), jnp.stack(vs_l), jnp.stack(cs_l))
```

```python
import functools
import math

import jax
import jax.numpy as jnp
from jax import lax
from jax.experimental import pallas as pl
from jax.experimental.pallas import tpu as pltpu

F32 = jnp.float32
BF16 = jnp.bfloat16

D_MODEL = 1024
N_META = 16
N_HEADS = 8
HEAD_DIM = 64
V_HEAD_DIM = 2 * HEAD_DIM
CONV_WIDTH = 31
CONV_STATE = CONV_WIDTH - 1
PAGE_SIZE = 128
PEER_HEADS = 8
N_KEYS = 128
PEER_TOPK = 16
PEER_HALF = 128
RMS_EPS = 1e-6
LN_EPS = 1e-5

VMEM_LIMIT_BYTES = 56 * 1024 * 1024

ROW_TILE = 512
ATT_TILE = 384
ATT_Q = 384
ATT_K = 384
CONV_TILE = 384
CONV_HALO = 32
PEER_TOKENS = 256
PEER_ROWS = 16
PEER_CHUNK = PEER_ROWS * N_KEYS
PEER_SUB = 2 * N_KEYS
PEER_LANES = 128
SAMPLE_CONV_SEQS = 16
SAMPLE_PAGES = 4

NEG_INF = float("-inf")


def _cparams(sem):
    return pltpu.CompilerParams(dimension_semantics=sem, vmem_limit_bytes=VMEM_LIMIT_BYTES)


def _rms(x, g):
    return x * lax.rsqrt(jnp.mean(x * x, axis=-1, keepdims=True) + RMS_EPS) * g


def _gelu(x):
    return 0.5 * x * (1.0 + lax.erf(x * math.sqrt(0.5)))


def _dot(a, b):
    return jnp.dot(a, b, preferred_element_type=F32)


def _dot_nt(a, b):
    return lax.dot_general(a, b, (((1,), (1,)), ((), ())), preferred_element_type=F32)


def _qkv_kernel(x_ref, g_ref, w_ref, q_ref, kb_ref, vb_ref, kf_ref, vf_ref):
    d = x_ref.shape[1]
    hb = _rms(x_ref[...], g_ref[...]).astype(BF16)
    q = _dot(hb, w_ref[:, 0:d])
    q_ref[...] = (q * (HEAD_DIM ** -0.5)).astype(BF16)
    k = _dot(hb, w_ref[:, d:2 * d])
    kf_ref[...] = k
    kb_ref[...] = k.astype(BF16)
    v = _dot(hb, w_ref[:, 2 * d:3 * d])
    vf_ref[...] = v
    vb_ref[...] = v.astype(BF16)


def _qkv(x, g, w):
    t, d = x.shape
    row = pl.BlockSpec((ROW_TILE, d), lambda i: (i, 0))
    return pl.pallas_call(
        _qkv_kernel,
        grid=(t // ROW_TILE,),
        in_specs=[row, pl.BlockSpec((1, d), lambda i: (0, 0)), pl.BlockSpec((d, 3 * d), lambda i: (0, 0))],
        out_specs=[row] * 5,
        out_shape=[jax.ShapeDtypeStruct((t, d), BF16)] * 3 + [jax.ShapeDtypeStruct((t, d), F32)] * 2,
        compiler_params=_cparams(("parallel",)),
        name="qkv",
    )(x, g.reshape(1, d), w)


def _lambda(lam_ref, lam_init):
    p = lam_ref[...]
    a = jnp.sum(p[0:1] * p[1:2], axis=-1, keepdims=True)
    b = jnp.sum(p[2:3] * p[3:4], axis=-1, keepdims=True)
    return jnp.exp(a) - jnp.exp(b) + lam_init


def _subln(o, g, lam_init):
    return _rms(o, g) * (1.0 - lam_init)


def _prompt_attn_kernel(q_ref, k_ref, v_ref, lam_ref, sg_ref, o_ref, *, lam_init):
    qi = pl.program_id(2)
    tq = q_ref.shape[0]
    tk = ATT_K
    q = q_ref[...]
    lane = lax.broadcasted_iota(jnp.int32, q.shape, 1)
    zero = jnp.zeros_like(q)
    qq = jnp.concatenate([jnp.where(lane < HEAD_DIM, q, zero), jnp.where(lane >= HEAD_DIM, q, zero)], axis=0)
    q0 = qi * tq

    def step(j, carry, masked):
        m, l, acc = carry
        start = pl.multiple_of(j * tk, tk)
        k = k_ref[pl.ds(start, tk), :]
        v = v_ref[pl.ds(start, tk), :]
        s = _dot_nt(qq, k)
        if masked:
            r = q0 + lax.broadcasted_iota(jnp.int32, (tq, tk), 0)
            c = start + lax.broadcasted_iota(jnp.int32, (tq, tk), 1)
            keep = jnp.concatenate([c <= r, c <= r], axis=0)
            s = jnp.where(keep, s, NEG_INF)
        m_new = jnp.maximum(m, jnp.max(s, axis=-1, keepdims=True))
        alpha = jnp.exp(m - m_new)
        p = jnp.exp(s - m_new)
        l = alpha * l + jnp.sum(p, axis=-1, keepdims=True)
        acc = alpha * acc + _dot(p.astype(BF16), v)
        return m_new, l, acc

    init = (jnp.full((2 * tq, 1), NEG_INF, F32), jnp.zeros((2 * tq, 1), F32), jnp.zeros((2 * tq, V_HEAD_DIM), F32))
    n_full = q0 // tk
    carry = lax.fori_loop(0, n_full // 2, lambda j, c: step(2 * j + 1, step(2 * j, c, False), False), init)
    carry = lax.cond(n_full % 2 == 1, lambda c: step(n_full - 1, c, False), lambda c: c, carry)
    for t in range(max(1, tq // tk)):
        carry = step(n_full + t, carry, True)
    m, l, acc = carry
    o = acc / l
    lam = _lambda(lam_ref, lam_init)
    o = o[:tq] - lam * o[tq:]
    o_ref[...] = _subln(o, sg_ref[...], lam_init).astype(o_ref.dtype)


def _prompt_attn(q, k, v, lam_params, subln_g, lam_init, n_batch, seq_pad):
    d = q.shape[1]
    assert seq_pad % ATT_Q == 0 and seq_pad % ATT_K == 0 and (ATT_K % ATT_Q == 0 or ATT_Q % ATT_K == 0)
    nq = seq_pad // ATT_Q
    qspec = pl.BlockSpec((ATT_Q, V_HEAD_DIM), lambda b, h, i: (b * nq + i, h))
    kvspec = pl.BlockSpec((seq_pad, V_HEAD_DIM), lambda b, h, i: (b, h))
    return pl.pallas_call(
        functools.partial(_prompt_attn_kernel, lam_init=lam_init),
        grid=(n_batch, N_HEADS, nq),
        in_specs=[qspec, kvspec, kvspec,
                  pl.BlockSpec((4, HEAD_DIM), lambda b, h, i: (0, 0)),
                  pl.BlockSpec((1, V_HEAD_DIM), lambda b, h, i: (0, 0))],
        out_specs=qspec,
        out_shape=jax.ShapeDtypeStruct((n_batch * seq_pad, d), BF16),
        compiler_params=_cparams(("parallel", "parallel", "arbitrary")),
        name="prompt_attn",
    )(q, k, v, lam_params, subln_g.reshape(1, V_HEAD_DIM))


def _sample_attn_kernel(pt_ref, q_ref, kn_ref, vn_ref, *refs, lam_init):
    k_refs = refs[:SAMPLE_PAGES]
    v_refs = refs[SAMPLE_PAGES:2 * SAMPLE_PAGES]
    ones_ref, lam_ref, sg_ref, o_ref, m_scr, l_scr, acc_scr = refs[2 * SAMPLE_PAGES:]
    p = pl.program_id(1)
    q = q_ref[0]
    ones = ones_ref[...]

    @pl.when(p == 0)
    def _():
        s = _dot((kn_ref[0] * q).astype(BF16), ones)
        m_scr[...] = s
        l_scr[...] = jnp.ones_like(s)
        vn = vn_ref[0]
        acc_scr[...] = jnp.concatenate([vn, vn], axis=1)

    scores = []
    for k_ref in k_refs:
        kp = k_ref[...]
        n_tok = kp.shape[0]
        prod = (kp * q[None]).reshape(n_tok * N_HEADS, V_HEAD_DIM).astype(BF16)
        scores.append(_dot(prod, ones).reshape(n_tok, N_HEADS, 2 * V_HEAD_DIM))
    m_old = m_scr[...]
    m_new = m_old
    for s in scores:
        m_new = jnp.maximum(m_new, jnp.max(s, axis=0))
    alpha = jnp.exp(m_old - m_new)
    l = alpha * l_scr[...]
    acc = alpha * acc_scr[...]
    for s, v_ref in zip(scores, v_refs):
        pe = jnp.exp(s - m_new[None])
        l = l + jnp.sum(pe, axis=0)
        vp = v_ref[...]
        acc = acc + jnp.concatenate([jnp.sum(pe[:, :, :V_HEAD_DIM] * vp, axis=0),
                                     jnp.sum(pe[:, :, V_HEAD_DIM:] * vp, axis=0)], axis=1)
    l_scr[...] = l
    acc_scr[...] = acc
    m_scr[...] = m_new

    @pl.when(p == pl.num_programs(1) - 1)
    def _():
        o = acc_scr[...] / l_scr[...]
        lam = _lambda(lam_ref, lam_init)
        o = o[:, :V_HEAD_DIM] - lam * o[:, V_HEAD_DIM:]
        o_ref[0] = _subln(o, sg_ref[...], lam_init)


def _sample_attn(page_table, q, k_new, v_new, k_pool, v_pool, layer, lam_params, subln_g, lam_init):
    n_seq, n_pages = page_table.shape
    assert n_pages % SAMPLE_PAGES == 0
    hd = (N_HEADS, V_HEAD_DIM)
    d_idx = jnp.arange(V_HEAD_DIM)[:, None] >= HEAD_DIM
    c_idx = jnp.arange(2 * V_HEAD_DIM)[None, :] >= V_HEAD_DIM
    ones = (d_idx == c_idx).astype(BF16)
    tok = pl.BlockSpec((1,) + hd, lambda s, p, pt: (s, 0, 0))

    def page(k):
        return pl.BlockSpec((None, None, PAGE_SIZE) + hd,
                            lambda s, p, pt: (layer, pt[s, p * SAMPLE_PAGES + k], 0, 0, 0))

    pages = [page(k) for k in range(SAMPLE_PAGES)]
    grid_spec = pltpu.PrefetchScalarGridSpec(
        num_scalar_prefetch=1,
        grid=(n_seq, n_pages // SAMPLE_PAGES),
        in_specs=[tok, tok, tok] + pages + pages +
                 [pl.BlockSpec((V_HEAD_DIM, 2 * V_HEAD_DIM), lambda s, p, pt: (0, 0)),
                  pl.BlockSpec((4, HEAD_DIM), lambda s, p, pt: (0, 0)),
                  pl.BlockSpec((1, V_HEAD_DIM), lambda s, p, pt: (0, 0))],
        out_specs=tok,
        scratch_shapes=[pltpu.VMEM((N_HEADS, 2 * V_HEAD_DIM), F32)] * 3,
    )
    return pl.pallas_call(
        functools.partial(_sample_attn_kernel, lam_init=lam_init),
        grid_spec=grid_spec,
        out_shape=jax.ShapeDtypeStruct((n_seq,) + hd, F32),
        compiler_params=_cparams(("parallel", "arbitrary")),
        name="sample_attn",
    )(page_table, q, k_new, v_new, *([k_pool] * SAMPLE_PAGES), *([v_pool] * SAMPLE_PAGES),
      ones, lam_params, subln_g.reshape(1, V_HEAD_DIM))


def _oproj_kernel(o_ref, w_ref, x_ref, out_ref):
    out_ref[...] = x_ref[...] + _dot(o_ref[...], w_ref[...])


def _oproj(o, w, x):
    t, d = x.shape
    row = pl.BlockSpec((ROW_TILE, d), lambda i: (i, 0))
    return pl.pallas_call(
        _oproj_kernel,
        grid=(t // ROW_TILE,),
        in_specs=[row, pl.BlockSpec((d, d), lambda i: (0, 0)), row],
        out_specs=row,
        out_shape=jax.ShapeDtypeStruct((t, d), F32),
        compiler_params=_cparams(("parallel",)),
        name="oproj",
    )(o, w, x)


def _glu_kernel(x_ref, g_ref, w_ref, b_ref, u_ref):
    d = x_ref.shape[1]
    hb = _rms(x_ref[...], g_ref[...]).astype(BF16)
    a = _dot(hb, w_ref[:, 0:d]) + b_ref[:, 0:d]
    gate = _dot(hb, w_ref[:, d:2 * d]) + b_ref[:, d:2 * d]
    u_ref[...] = a * jax.nn.sigmoid(gate)


def _glu(x, g, w, b):
    t, d = x.shape
    row = pl.BlockSpec((ROW_TILE, d), lambda i: (i, 0))
    return pl.pallas_call(
        _glu_kernel,
        grid=(t // ROW_TILE,),
        in_specs=[row, pl.BlockSpec((1, d), lambda i: (0, 0)), pl.BlockSpec((d, 2 * d), lambda i: (0, 0)),
                  pl.BlockSpec((1, 2 * d), lambda i: (0, 0))],
        out_specs=row,
        out_shape=jax.ShapeDtypeStruct((t, d), F32),
        compiler_params=_cparams(("parallel",)),
        name="glu",
    )(x, g.reshape(1, d), w, b.reshape(1, 2 * d))


def _conv_tail(c, lng_ref, lnb_ref, w2_ref, x_ref, out_ref):
    mu = jnp.mean(c, axis=-1, keepdims=True)
    cc = c - mu
    var = jnp.mean(cc * cc, axis=-1, keepdims=True)
    cn = cc * lax.rsqrt(var + LN_EPS) * lng_ref[...] + lnb_ref[...]
    act = (cn * jax.nn.sigmoid(cn)).astype(BF16)
    out_ref[...] = x_ref[...] + _dot(act, w2_ref[...])


def _conv_prompt_kernel(u_ref, halo_ref, wdw_ref, bdw_ref, lng_ref, lnb_ref, w2_ref, x_ref, out_ref, buf,
                        *, tiles_per_seq):
    i = pl.program_id(0)
    tile = u_ref.shape[0]
    halo = halo_ref[...]
    buf[0:CONV_HALO, :] = jnp.where(i % tiles_per_seq == 0, jnp.zeros_like(halo), halo)
    buf[CONV_HALO:CONV_HALO + tile, :] = u_ref[...]
    off = CONV_HALO - CONV_STATE
    c = buf[off:off + tile, :] * wdw_ref[0:1, :]
    for w in range(1, CONV_WIDTH):
        c = c + buf[off + w:off + w + tile, :] * wdw_ref[w:w + 1, :]
    _conv_tail(c + bdw_ref[...], lng_ref, lnb_ref, w2_ref, x_ref, out_ref)


def _conv_prompt(u, x, w_dw, b_dw, ln_g, ln_b, w2, n_rows, seq_pad):
    d = u.shape[1]
    per_tile = CONV_TILE // CONV_HALO
    row = pl.BlockSpec((CONV_TILE, d), lambda i: (i, 0))
    vec = pl.BlockSpec((1, d), lambda i: (0, 0))
    return pl.pallas_call(
        functools.partial(_conv_prompt_kernel, tiles_per_seq=seq_pad // CONV_TILE),
        grid=(n_rows // CONV_TILE,),
        in_specs=[row, pl.BlockSpec((CONV_HALO, d), lambda i: (jnp.maximum(i * per_tile - 1, 0), 0)),
                  pl.BlockSpec((CONV_WIDTH, d), lambda i: (0, 0)), vec, vec, vec,
                  pl.BlockSpec((d, d), lambda i: (0, 0)), row],
        out_specs=row,
        out_shape=jax.ShapeDtypeStruct((n_rows, d), F32),
        scratch_shapes=[pltpu.VMEM((CONV_HALO + CONV_TILE, d), F32)],
        compiler_params=_cparams(("parallel",)),
        name="conv_prompt",
    )(u, u, w_dw, b_dw.reshape(1, d), ln_g.reshape(1, d), ln_b.reshape(1, d), w2, x)


def _conv_sample_kernel(st_ref, u_ref, wdw_ref, bdw_ref, lng_ref, lnb_ref, w2_ref, x_ref, out_ref):
    c = jnp.sum(st_ref[...] * wdw_ref[0:CONV_STATE, :][None], axis=1)
    c = c + u_ref[...] * wdw_ref[CONV_STATE:CONV_WIDTH, :] + bdw_ref[...]
    _conv_tail(c, lng_ref, lnb_ref, w2_ref, x_ref, out_ref)


def _conv_sample(state, u, x, w_dw, b_dw, ln_g, ln_b, w2, first_row):
    n_seq, _, d = state.shape
    ns = SAMPLE_CONV_SEQS
    base = first_row // ns
    row = pl.BlockSpec((ns, d), lambda i: (base + i, 0))
    vec = pl.BlockSpec((1, d), lambda i: (0, 0))
    return pl.pallas_call(
        _conv_sample_kernel,
        grid=(n_seq // ns,),
        in_specs=[pl.BlockSpec((ns, CONV_STATE, d), lambda i: (i, 0, 0)), row,
                  pl.BlockSpec((CONV_WIDTH, d), lambda i: (0, 0)), vec, vec, vec,
                  pl.BlockSpec((d, d), lambda i: (0, 0)), row],
        out_specs=pl.BlockSpec((ns, d), lambda i: (i, 0)),
        out_shape=jax.ShapeDtypeStruct((n_seq, d), F32),
        compiler_params=_cparams(("parallel",)),
        name="conv_sample",
    )(state, u, w_dw, b_dw.reshape(1, d), ln_g.reshape(1, d), ln_b.reshape(1, d), w2, x)


def _top_values(s, n):
    rows = []
    work = s
    for r in range(n):
        m = jnp.max(work, axis=0, keepdims=True)
        rows.append(m)
        if r + 1 < n:
            work = jnp.where(work == m, NEG_INF, work)
    return rows


def _pair_candidates(t1, t2):
    rows = [t1[a] + t2[b] for a in range(PEER_TOPK) for b in range(PEER_TOPK // (a + 1))]
    pad = -len(rows) % 8
    rows += [jnp.full_like(rows[0], NEG_INF)] * pad
    return jnp.concatenate(rows, axis=0)


def _peer_kernel(x_ref, g_ref, wq_ref, keys_ref, u_ref, vt_ref, o_ref,
                 h_scr, q_scr, th_scr, s2_scr, e2_scr, c1_scr, acc_scr):
    c = pl.program_id(1)
    tb = x_ref.shape[0]

    @pl.when(c == 0)
    def _():
        hb = _rms(x_ref[...], g_ref[...]).astype(BF16)
        h_scr[...] = hb
        q_scr[...] = _dot_nt(wq_ref[...], hb)

        def head(hd, carry):
            base = pl.multiple_of(hd * (2 * PEER_HALF), 2 * PEER_HALF)
            q1 = q_scr[pl.ds(base, PEER_HALF), :].astype(BF16)
            q2 = q_scr[pl.ds(base + PEER_HALF, PEER_HALF), :].astype(BF16)
            s1 = _dot(keys_ref[hd, 0], q1)
            s2 = _dot(keys_ref[hd, 1], q2)
            t1 = _top_values(s1, PEER_TOPK)
            t2 = _top_values(s2, PEER_TOPK)
            cand = _pair_candidates(t1, t2)
            tau = _top_values(cand, PEER_TOPK)[-1]
            top = t1[0] + t2[0]
            z = jnp.sum(jnp.where(cand >= tau, jnp.exp(cand - top), 0.0), axis=0, keepdims=True)
            c1 = jnp.exp(s1 - t1[0]) * (1.0 / z)
            e2 = jnp.exp(s2 - t2[0])
            th = jnp.full(s1.shape, jnp.inf, F32)
            for a in range(PEER_TOPK):
                th_a = jnp.full_like(tau, jnp.inf)
                for b in range(PEER_TOPK // (a + 1)):
                    th_a = jnp.minimum(th_a, jnp.where(t1[a] + t2[b] >= tau, t2[b], jnp.inf))
                th = jnp.where(s1 == t1[a], th_a, th)
            for lt in range(tb // PEER_LANES):
                ls = slice(lt * PEER_LANES, (lt + 1) * PEER_LANES)
                th_scr[hd, lt] = th[:, ls]
                s2_scr[hd, lt] = s2[:, ls]
                c1_scr[hd, lt] = c1[:, ls]
                e2_scr[hd, lt] = e2[:, ls]
            return carry

        lax.fori_loop(0, PEER_HEADS, head, 0, unroll=4)
        acc_scr[...] = jnp.zeros_like(acc_scr)

    hb = h_scr[...]
    rows_per_sub = PEER_SUB // N_KEYS
    for r in range(PEER_CHUNK // PEER_SUB):
        ga = _gelu(_dot_nt(u_ref[r * PEER_SUB:(r + 1) * PEER_SUB, :], hb))
        w_rows = []
        for kr in range(rows_per_sub):
            i1 = c * PEER_ROWS + r * rows_per_sub + kr
            cols = []
            for lt in range(tb // PEER_LANES):
                g = None
                for hd in range(PEER_HEADS):
                    keep = s2_scr[hd, lt] >= th_scr[hd, lt, pl.ds(i1, 1), :]
                    term = jnp.where(keep, e2_scr[hd, lt], 0.0) * c1_scr[hd, lt, pl.ds(i1, 1), :]
                    g = term if g is None else g + term
                ls = slice(lt * PEER_LANES, (lt + 1) * PEER_LANES)
                cols.append((g * ga[kr * N_KEYS:(kr + 1) * N_KEYS, ls]).astype(BF16))
            w_rows.append(jnp.concatenate(cols, axis=1))
        w = jnp.concatenate(w_rows, axis=0)
        acc_scr[...] += _dot(vt_ref[:, r * PEER_SUB:(r + 1) * PEER_SUB], w)

    @pl.when(c == pl.num_programs(1) - 1)
    def _():
        o_ref[...] = x_ref[...] + acc_scr[...].T


def _peer(x, g, wq_t, keys, u, v_t):
    t, d = x.shape
    tb = PEER_TOKENS
    n_exp = u.shape[0]
    row = pl.BlockSpec((tb, d), lambda i, c: (i, 0))
    tile_shape = (PEER_HEADS, tb // PEER_LANES, N_KEYS, PEER_LANES)
    return pl.pallas_call(
        _peer_kernel,
        grid=(t // tb, n_exp // PEER_CHUNK),
        in_specs=[row, pl.BlockSpec((1, d), lambda i, c: (0, 0)),
                  pl.BlockSpec(wq_t.shape, lambda i, c: (0, 0)),
                  pl.BlockSpec(keys.shape, lambda i, c: (0, 0, 0, 0)),
                  pl.BlockSpec((PEER_CHUNK, d), lambda i, c: (c, 0)),
                  pl.BlockSpec((d, PEER_CHUNK), lambda i, c: (0, c))],
        out_specs=row,
        out_shape=jax.ShapeDtypeStruct((t, d), F32),
        scratch_shapes=[pltpu.VMEM((tb, d), BF16),
                        pltpu.VMEM((PEER_HEADS * 2 * PEER_HALF, tb), F32),
                        pltpu.VMEM(tile_shape, F32),
                        pltpu.VMEM(tile_shape, F32),
                        pltpu.VMEM(tile_shape, F32),
                        pltpu.VMEM(tile_shape, F32),
                        pltpu.VMEM((d, tb), F32)],
        compiler_params=_cparams(("parallel", "arbitrary")),
        name="peer",
    )(x, g.reshape(1, d), wq_t, keys, u, v_t)


def _final_norm_kernel(x_ref, g_ref, y_ref):
    y_ref[...] = _rms(x_ref[...], g_ref[...])


def _final_norm(x, g):
    t, d = x.shape
    row = pl.BlockSpec((ROW_TILE, d), lambda i: (i, 0))
    return pl.pallas_call(
        _final_norm_kernel,
        grid=(t // ROW_TILE,),
        in_specs=[row, pl.BlockSpec((1, d), lambda i: (0, 0))],
        out_specs=row,
        out_shape=jax.ShapeDtypeStruct((t, d), F32),
        compiler_params=_cparams(("parallel",)),
        name="final_norm",
    )(x, g.reshape(1, d))


def kernel(x_prompt, x_sample, cache_k, cache_v, state_conv, page_table, meta_tokens, g_mix, g_ffn, g_final, w_qkv, lambda_q1, lambda_k1, lambda_q2, lambda_k2, subln_g, w_o_attn, w_pw1, b_pw1, w_dw, b_dw, ln_g, ln_b, w_pw2, peer_w_query, peer_sub_keys, peer_u, peer_v):
    n_batch, seq, d = x_prompt.shape
    n_dec = x_sample.shape[0]
    depth = g_mix.shape[0]
    seq_len = seq + N_META
    seq_pad = -(-seq_len // ATT_TILE) * ATT_TILE
    n_prompt_rows = n_batch * seq_pad
    assert seq_pad % CONV_TILE == 0 and n_prompt_rows % SAMPLE_CONV_SEQS == 0
    assert x_sample.shape[1] == 1 and n_dec % SAMPLE_CONV_SEQS == 0
    t_total = -(-(n_prompt_rows + n_dec) // ROW_TILE) * ROW_TILE
    assert t_total % PEER_TOKENS == 0
    n_tail = t_total - n_prompt_rows - n_dec

    def with_sample_rows(prompt_rows, sample_rows):
        return jnp.concatenate([prompt_rows, sample_rows, jnp.zeros((n_tail, d), prompt_rows.dtype)], axis=0)

    meta = jnp.broadcast_to(meta_tokens.astype(F32)[None], (n_batch, N_META, d))
    xp = jnp.concatenate([meta, x_prompt, jnp.zeros((n_batch, seq_pad - seq_len, d), F32)], axis=1)
    x = with_sample_rows(xp.reshape(n_prompt_rows, d), x_sample.reshape(n_dec, d))

    def prompt_part(a):
        return a[:n_prompt_rows].reshape(n_batch, seq_pad, -1)[:, :seq_len]

    def sample_part(a):
        return a[n_prompt_rows:n_prompt_rows + n_dec]

    kp_l, vp_l, cp_l, ks_l, vs_l, cs_l = [], [], [], [], [], []
    for i in range(depth):
        j = i // 2
        if i % 2 == 0:
            lam_init = 0.8 - 0.6 * math.exp(-0.3 * i)
            lam_params = jnp.stack([lambda_q1[j], lambda_k1[j], lambda_q2[j], lambda_k2[j]])
            qs, kb, vb, kf, vf = _qkv(x, g_mix[i], w_qkv[j].astype(BF16))
            o_p = _prompt_attn(qs, kb, vb, lam_params, subln_g[j], lam_init, n_batch, seq_pad)
            heads = (n_dec, N_HEADS, V_HEAD_DIM)
            o_s = _sample_attn(page_table, sample_part(qs).astype(F32).reshape(heads),
                               sample_part(kf).reshape(heads), sample_part(vf).reshape(heads),
                               cache_k, cache_v, j, lam_params, subln_g[j], lam_init)
            o = with_sample_rows(o_p, o_s.reshape(n_dec, d).astype(BF16))
            x = _oproj(o, w_o_attn[j].astype(BF16), x)
            kv_shape = (n_batch, seq_len, N_HEADS, V_HEAD_DIM)
            kp_l.append(prompt_part(kf).reshape(kv_shape))
            vp_l.append(prompt_part(vf).reshape(kv_shape))
            ks_l.append(sample_part(kf).reshape(n_dec, 1, N_HEADS, V_HEAD_DIM))
            vs_l.append(sample_part(vf).reshape(n_dec, 1, N_HEADS, V_HEAD_DIM))
        else:
            u = _glu(x, g_mix[i], w_pw1[j].astype(BF16), b_pw1[j])
            w2 = w_pw2[j].astype(BF16)
            x_p = _conv_prompt(u, x, w_dw[j], b_dw[j], ln_g[j], ln_b[j], w2, n_prompt_rows, seq_pad)
            x_s = _conv_sample(state_conv[j], u, x, w_dw[j], b_dw[j], ln_g[j], ln_b[j], w2, n_prompt_rows)
            x = with_sample_rows(x_p, x_s)
            cp_l.append(prompt_part(u)[:, seq_len - CONV_STATE:])
            cs_l.append(jnp.concatenate([state_conv[j][:, 1:], sample_part(u)[:, None]], axis=1))
        x = _peer(x, g_ffn[i], peer_w_query[i].T.astype(BF16), peer_sub_keys[i].astype(BF16),
                  peer_u[i].astype(BF16), peer_v[i].T.astype(BF16))
    y = _final_norm(x, g_final)
    return (prompt_part(y)[:, N_META:], sample_part(y).reshape(n_dec, 1, d),
            jnp.stack(kp_l), jnp.stack(vp_l), jnp.stack(cp_l),
            jnp.stack(ks_l), jnp.stack(vs_l), jnp.stack(cs_l))
```

```python
import functools
import math

import jax
import jax.numpy as jnp
from jax import lax
from jax.experimental import pallas as pl
from jax.experimental.pallas import tpu as pltpu

F32 = jnp.float32
BF16 = jnp.bfloat16

D_MODEL = 1024
N_META = 16
N_HEADS = 8
HEAD_DIM = 64
V_HEAD_DIM = 2 * HEAD_DIM
CONV_WIDTH = 31
CONV_STATE = CONV_WIDTH - 1
PAGE_SIZE = 128
PEER_HEADS = 8
N_KEYS = 128
PEER_TOPK = 16
PEER_HALF = 128
RMS_EPS = 1e-6
LN_EPS = 1e-5

SUBLANES = 8
LANES = 128
VMEM_LIMIT_BYTES = 56 * 1024 * 1024

ROW_TILE = 512
ATT_TILE = 384
ATT_Q = 384
ATT_K = 384
CONV_TILE = 384
CONV_HALO = 32
PEER_TOKENS = 256
PEER_ROWS = 16
PEER_CHUNK = PEER_ROWS * N_KEYS
PEER_SUB = 2 * N_KEYS
PEER_LANES = 128
SAMPLE_CONV_SEQS = 16
SAMPLE_PAGES = 4

NEG_INF = float("-inf")


def _cparams(sem):
    return pltpu.CompilerParams(dimension_semantics=sem, vmem_limit_bytes=VMEM_LIMIT_BYTES)


def _rms(x, g):
    return x * lax.rsqrt(jnp.mean(x * x, axis=-1, keepdims=True) + RMS_EPS) * g


def _gelu(x):
    return 0.5 * x * (1.0 + lax.erf(x * math.sqrt(0.5)))


def _dot(a, b):
    return jnp.dot(a, b, preferred_element_type=F32)


def _dot_nt(a, b):
    return lax.dot_general(a, b, (((1,), (1,)), ((), ())), preferred_element_type=F32)


def _qkv_kernel(x_ref, g_ref, w_ref, q_ref, kb_ref, vb_ref, kf_ref, vf_ref):
    d = x_ref.shape[1]
    hb = _rms(x_ref[...], g_ref[...]).astype(BF16)
    q = _dot(hb, w_ref[:, 0:d])
    q_ref[...] = (q * (HEAD_DIM ** -0.5)).astype(BF16)
    k = _dot(hb, w_ref[:, d:2 * d])
    kf_ref[...] = k
    kb_ref[...] = k.astype(BF16)
    v = _dot(hb, w_ref[:, 2 * d:3 * d])
    vf_ref[...] = v
    vb_ref[...] = v.astype(BF16)


def _qkv(x, g, w):
    t, d = x.shape
    row = pl.BlockSpec((ROW_TILE, d), lambda i: (i, 0))
    return pl.pallas_call(
        _qkv_kernel,
        grid=(t // ROW_TILE,),
        in_specs=[row, pl.BlockSpec((1, d), lambda i: (0, 0)), pl.BlockSpec((d, 3 * d), lambda i: (0, 0))],
        out_specs=[row] * 5,
        out_shape=[jax.ShapeDtypeStruct((t, d), BF16)] * 3 + [jax.ShapeDtypeStruct((t, d), F32)] * 2,
        compiler_params=_cparams(("parallel",)),
        name="qkv",
    )(x, g.reshape(1, d), w)


def _lambda(lam_ref, lam_init):
    p = lam_ref[...]
    a = jnp.sum(p[0:1] * p[1:2], axis=-1, keepdims=True)
    b = jnp.sum(p[2:3] * p[3:4], axis=-1, keepdims=True)
    return jnp.exp(a) - jnp.exp(b) + lam_init


def _subln(o, g, lam_init):
    return _rms(o, g) * (1.0 - lam_init)


def _prompt_attn_kernel(q_ref, k_ref, v_ref, lam_ref, sg_ref, o_ref, *, lam_init):
    qi = pl.program_id(2)
    tq = q_ref.shape[0]
    tk = ATT_K
    q = q_ref[...]
    lane = lax.broadcasted_iota(jnp.int32, q.shape, 1)
    zero = jnp.zeros_like(q)
    qq = jnp.concatenate([jnp.where(lane < HEAD_DIM, q, zero), jnp.where(lane >= HEAD_DIM, q, zero)], axis=0)
    q0 = qi * tq

    def step(j, carry, masked):
        m, l, acc = carry
        start = pl.multiple_of(j * tk, tk)
        k = k_ref[pl.ds(start, tk), :]
        v = v_ref[pl.ds(start, tk), :]
        s = _dot_nt(qq, k)
        if masked:
            r = q0 + lax.broadcasted_iota(jnp.int32, (tq, tk), 0)
            c = start + lax.broadcasted_iota(jnp.int32, (tq, tk), 1)
            keep = jnp.concatenate([c <= r, c <= r], axis=0)
            s = jnp.where(keep, s, NEG_INF)
        m_new = jnp.maximum(m, jnp.max(s, axis=-1, keepdims=True))
        alpha = jnp.exp(m - m_new)
        p = jnp.exp(s - m_new)
        l = alpha * l + jnp.sum(p, axis=-1, keepdims=True)
        acc = alpha * acc + _dot(p.astype(BF16), v)
        return m_new, l, acc

    init = (jnp.full((2 * tq, 1), NEG_INF, F32), jnp.zeros((2 * tq, 1), F32), jnp.zeros((2 * tq, V_HEAD_DIM), F32))
    n_full = q0 // tk
    carry = lax.fori_loop(0, n_full // 2, lambda j, c: step(2 * j + 1, step(2 * j, c, False), False), init)
    carry = lax.cond(n_full % 2 == 1, lambda c: step(n_full - 1, c, False), lambda c: c, carry)
    for t in range(max(1, tq // tk)):
        carry = step(n_full + t, carry, True)
    m, l, acc = carry
    o = acc / l
    lam = _lambda(lam_ref, lam_init)
    o = o[:tq] - lam * o[tq:]
    o_ref[...] = _subln(o, sg_ref[...], lam_init).astype(o_ref.dtype)


def _prompt_attn(q, k, v, lam_params, subln_g, lam_init, n_batch, seq_pad):
    d = q.shape[1]
    assert seq_pad % ATT_Q == 0 and seq_pad % ATT_K == 0 and (ATT_K % ATT_Q == 0 or ATT_Q % ATT_K == 0)
    nq = seq_pad // ATT_Q
    qspec = pl.BlockSpec((ATT_Q, V_HEAD_DIM), lambda b, h, i: (b * nq + i, h))
    kvspec = pl.BlockSpec((seq_pad, V_HEAD_DIM), lambda b, h, i: (b, h))
    return pl.pallas_call(
        functools.partial(_prompt_attn_kernel, lam_init=lam_init),
        grid=(n_batch, N_HEADS, nq),
        in_specs=[qspec, kvspec, kvspec,
                  pl.BlockSpec((4, HEAD_DIM), lambda b, h, i: (0, 0)),
                  pl.BlockSpec((1, V_HEAD_DIM), lambda b, h, i: (0, 0))],
        out_specs=qspec,
        out_shape=jax.ShapeDtypeStruct((n_batch * seq_pad, d), BF16),
        compiler_params=_cparams(("parallel", "parallel", "arbitrary")),
        name="prompt_attn",
    )(q, k, v, lam_params, subln_g.reshape(1, V_HEAD_DIM))


def _sample_attn_kernel(pt_ref, q_ref, kn_ref, vn_ref, *refs, lam_init):
    k_refs = refs[:SAMPLE_PAGES]
    v_refs = refs[SAMPLE_PAGES:2 * SAMPLE_PAGES]
    ones_ref, lam_ref, sg_ref, o_ref, m_scr, l_scr, acc_scr = refs[2 * SAMPLE_PAGES:]
    p = pl.program_id(1)
    q = q_ref[0]
    ones = ones_ref[...]

    @pl.when(p == 0)
    def _():
        s = _dot((kn_ref[0] * q).astype(BF16), ones)
        m_scr[...] = s
        l_scr[...] = jnp.ones_like(s)
        vn = vn_ref[0]
        acc_scr[...] = jnp.concatenate([vn, vn], axis=1)

    scores = []
    for k_ref in k_refs:
        kp = k_ref[...]
        n_tok = kp.shape[0]
        prod = (kp * q[None]).reshape(n_tok * N_HEADS, V_HEAD_DIM).astype(BF16)
        scores.append(_dot(prod, ones).reshape(n_tok, N_HEADS, 2 * V_HEAD_DIM))
    m_old = m_scr[...]
    m_new = m_old
    for s in scores:
        m_new = jnp.maximum(m_new, jnp.max(s, axis=0))
    alpha = jnp.exp(m_old - m_new)
    l = alpha * l_scr[...]
    acc = alpha * acc_scr[...]
    for s, v_ref in zip(scores, v_refs):
        pe = jnp.exp(s - m_new[None])
        l = l + jnp.sum(pe, axis=0)
        vp = v_ref[...]
        acc = acc + jnp.concatenate([jnp.sum(pe[:, :, :V_HEAD_DIM] * vp, axis=0),
                                     jnp.sum(pe[:, :, V_HEAD_DIM:] * vp, axis=0)], axis=1)
    l_scr[...] = l
    acc_scr[...] = acc
    m_scr[...] = m_new

    @pl.when(p == pl.num_programs(1) - 1)
    def _():
        o = acc_scr[...] / l_scr[...]
        lam = _lambda(lam_ref, lam_init)
        o = o[:, :V_HEAD_DIM] - lam * o[:, V_HEAD_DIM:]
        o_ref[0] = _subln(o, sg_ref[...], lam_init)


def _sample_attn(page_table, q, k_new, v_new, k_pool, v_pool, layer, lam_params, subln_g, lam_init):
    n_seq, n_pages = page_table.shape
    assert n_pages % SAMPLE_PAGES == 0
    hd = (N_HEADS, V_HEAD_DIM)
    d_idx = jnp.arange(V_HEAD_DIM)[:, None] >= HEAD_DIM
    c_idx = jnp.arange(2 * V_HEAD_DIM)[None, :] >= V_HEAD_DIM
    ones = (d_idx == c_idx).astype(BF16)
    tok = pl.BlockSpec((1,) + hd, lambda s, p, pt: (s, 0, 0))

    def page(k):
        return pl.BlockSpec((None, None, PAGE_SIZE) + hd,
                            lambda s, p, pt: (layer, pt[s, p * SAMPLE_PAGES + k], 0, 0, 0))

    pages = [page(k) for k in range(SAMPLE_PAGES)]
    grid_spec = pltpu.PrefetchScalarGridSpec(
        num_scalar_prefetch=1,
        grid=(n_seq, n_pages // SAMPLE_PAGES),
        in_specs=[tok, tok, tok] + pages + pages +
                 [pl.BlockSpec((V_HEAD_DIM, 2 * V_HEAD_DIM), lambda s, p, pt: (0, 0)),
                  pl.BlockSpec((4, HEAD_DIM), lambda s, p, pt: (0, 0)),
                  pl.BlockSpec((1, V_HEAD_DIM), lambda s, p, pt: (0, 0))],
        out_specs=tok,
        scratch_shapes=[pltpu.VMEM((N_HEADS, 2 * V_HEAD_DIM), F32)] * 3,
    )
    return pl.pallas_call(
        functools.partial(_sample_attn_kernel, lam_init=lam_init),
        grid_spec=grid_spec,
        out_shape=jax.ShapeDtypeStruct((n_seq,) + hd, F32),
        compiler_params=_cparams(("parallel", "arbitrary")),
        name="sample_attn",
    )(page_table, q, k_new, v_new, *([k_pool] * SAMPLE_PAGES), *([v_pool] * SAMPLE_PAGES),
      ones, lam_params, subln_g.reshape(1, V_HEAD_DIM))


def _oproj_kernel(om_ref, ot_ref, w_ref, x_ref, out_ref, *, n_main):
    o = jnp.where(pl.program_id(0) < n_main, om_ref[...], ot_ref[...])
    out_ref[...] = x_ref[...] + _dot(o, w_ref[...])


def _oproj(o_main, o_tail, w, x):
    t, d = x.shape
    n_main = o_main.shape[0] // ROW_TILE
    assert o_main.shape[0] % ROW_TILE == 0 and o_tail.shape[0] == ROW_TILE and t == (n_main + 1) * ROW_TILE
    row = pl.BlockSpec((ROW_TILE, d), lambda i: (i, 0))
    return pl.pallas_call(
        functools.partial(_oproj_kernel, n_main=n_main),
        grid=(t // ROW_TILE,),
        in_specs=[pl.BlockSpec((ROW_TILE, d), lambda i: (jnp.minimum(i, n_main - 1), 0)),
                  pl.BlockSpec((ROW_TILE, d), lambda i: (0, 0)),
                  pl.BlockSpec((d, d), lambda i: (0, 0)), row],
        out_specs=row,
        out_shape=jax.ShapeDtypeStruct((t, d), F32),
        compiler_params=_cparams(("parallel",)),
        name="oproj",
    )(o_main, o_tail, w, x)


def _glu_kernel(x_ref, g_ref, w_ref, b_ref, u_ref):
    d = x_ref.shape[1]
    hb = _rms(x_ref[...], g_ref[...]).astype(BF16)
    a = _dot(hb, w_ref[:, 0:d]) + b_ref[:, 0:d]
    gate = _dot(hb, w_ref[:, d:2 * d]) + b_ref[:, d:2 * d]
    u_ref[...] = a * jax.nn.sigmoid(gate)


def _glu(x, g, w, b):
    t, d = x.shape
    row = pl.BlockSpec((ROW_TILE, d), lambda i: (i, 0))
    return pl.pallas_call(
        _glu_kernel,
        grid=(t // ROW_TILE,),
        in_specs=[row, pl.BlockSpec((1, d), lambda i: (0, 0)), pl.BlockSpec((d, 2 * d), lambda i: (0, 0)),
                  pl.BlockSpec((1, 2 * d), lambda i: (0, 0))],
        out_specs=row,
        out_shape=jax.ShapeDtypeStruct((t, d), F32),
        compiler_params=_cparams(("parallel",)),
        name="glu",
    )(x, g.reshape(1, d), w, b.reshape(1, 2 * d))


def _conv_tail(c, lng_ref, lnb_ref, w2_ref, x_ref, out_ref):
    mu = jnp.mean(c, axis=-1, keepdims=True)
    cc = c - mu
    var = jnp.mean(cc * cc, axis=-1, keepdims=True)
    cn = cc * lax.rsqrt(var + LN_EPS) * lng_ref[...] + lnb_ref[...]
    act = (cn * jax.nn.sigmoid(cn)).astype(BF16)
    out_ref[...] = x_ref[...] + _dot(act, w2_ref[...])


def _conv_prompt_kernel(u_ref, halo_ref, wdw_ref, bdw_ref, lng_ref, lnb_ref, w2_ref, x_ref, out_ref,
                        buf, shift_scr, c_scr, *, tiles_per_seq):
    i = pl.program_id(0)
    tile, d = u_ref.shape
    halo = halo_ref[...]
    buf[0:CONV_HALO, :] = jnp.where(i % tiles_per_seq == 0, jnp.zeros_like(halo), halo)
    buf[CONV_HALO:CONV_HALO + tile, :] = u_ref[...]
    n_shift = shift_scr.shape[1]
    for s in range(1, SUBLANES):
        shift_scr[s - 1] = buf[s:s + n_shift, :]
    off = CONV_HALO - CONV_STATE
    for l0 in range(0, d, LANES):
        ls = slice(l0, l0 + LANES)
        c = None
        for w in range(CONV_WIDTH):
            s = (off + w) % SUBLANES
            a = off + w - s
            rows = buf[a:a + tile, ls] if s == 0 else shift_scr[s - 1, a:a + tile, ls]
            term = rows * wdw_ref[w:w + 1, ls]
            c = term if c is None else c + term
        c_scr[:, ls] = c + bdw_ref[:, ls]
    _conv_tail(c_scr[...], lng_ref, lnb_ref, w2_ref, x_ref, out_ref)


def _conv_prompt(u, x, w_dw, b_dw, ln_g, ln_b, w2, n_rows, seq_pad):
    d = u.shape[1]
    per_tile = CONV_TILE // CONV_HALO
    row = pl.BlockSpec((CONV_TILE, d), lambda i: (i, 0))
    vec = pl.BlockSpec((1, d), lambda i: (0, 0))
    return pl.pallas_call(
        functools.partial(_conv_prompt_kernel, tiles_per_seq=seq_pad // CONV_TILE),
        grid=(n_rows // CONV_TILE,),
        in_specs=[row, pl.BlockSpec((CONV_HALO, d), lambda i: (jnp.maximum(i * per_tile - 1, 0), 0)),
                  pl.BlockSpec((CONV_WIDTH, d), lambda i: (0, 0)), vec, vec, vec,
                  pl.BlockSpec((d, d), lambda i: (0, 0)), row],
        out_specs=row,
        out_shape=jax.ShapeDtypeStruct((n_rows, d), F32),
        scratch_shapes=[pltpu.VMEM((CONV_HALO + CONV_TILE, d), F32),
                        pltpu.VMEM((SUBLANES - 1, CONV_HALO + CONV_TILE - SUBLANES, d), F32),
                        pltpu.VMEM((CONV_TILE, d), F32)],
        compiler_params=_cparams(("parallel",)),
        name="conv_prompt",
    )(u, u, w_dw, b_dw.reshape(1, d), ln_g.reshape(1, d), ln_b.reshape(1, d), w2, x)


def _conv_sample_kernel(st_ref, u_ref, wdw_ref, bdw_ref, lng_ref, lnb_ref, w2_ref, x_ref, out_ref):
    c = jnp.sum(st_ref[...] * wdw_ref[0:CONV_STATE, :][None], axis=1)
    c = c + u_ref[...] * wdw_ref[CONV_STATE:CONV_WIDTH, :] + bdw_ref[...]
    _conv_tail(c, lng_ref, lnb_ref, w2_ref, x_ref, out_ref)


def _conv_sample(state, u, x, w_dw, b_dw, ln_g, ln_b, w2, first_row):
    n_seq, _, d = state.shape
    ns = SAMPLE_CONV_SEQS
    base = first_row // ns
    row = pl.BlockSpec((ns, d), lambda i: (base + i, 0))
    vec = pl.BlockSpec((1, d), lambda i: (0, 0))
    return pl.pallas_call(
        _conv_sample_kernel,
        grid=(n_seq // ns,),
        in_specs=[pl.BlockSpec((ns, CONV_STATE, d), lambda i: (i, 0, 0)), row,
                  pl.BlockSpec((CONV_WIDTH, d), lambda i: (0, 0)), vec, vec, vec,
                  pl.BlockSpec((d, d), lambda i: (0, 0)), row],
        out_specs=pl.BlockSpec((ns, d), lambda i: (i, 0)),
        out_shape=jax.ShapeDtypeStruct((n_seq, d), F32),
        compiler_params=_cparams(("parallel",)),
        name="conv_sample",
    )(state, u, w_dw, b_dw.reshape(1, d), ln_g.reshape(1, d), ln_b.reshape(1, d), w2, x)


def _top_values(s, n):
    rows = []
    work = s
    for r in range(n):
        m = jnp.max(work, axis=0, keepdims=True)
        rows.append(m)
        if r + 1 < n:
            work = jnp.where(work == m, NEG_INF, work)
    return rows


def _sort16_network():
    def merge(lo, hi, r):
        step = r * 2
        if step < hi - lo:
            yield from merge(lo, hi, step)
            yield from merge(lo + r, hi, step)
            yield from [(i, i + r) for i in range(lo + r, hi - r, step)]
        else:
            yield (lo, lo + r)

    def sort(lo, hi):
        if hi - lo >= 1:
            mid = lo + (hi - lo) // 2
            yield from sort(lo, mid)
            yield from sort(mid + 1, hi)
            yield from merge(lo, hi, 1)

    return tuple(sort(0, PEER_TOPK - 1))


def _top16_sorted(s):
    n = PEER_TOPK
    assert s.shape[0] == n * SUBLANES
    v = [s[SUBLANES * k:SUBLANES * (k + 1), :] for k in range(n)]

    def exchange(i, j):
        v[i], v[j] = jnp.maximum(v[i], v[j]), jnp.minimum(v[i], v[j])

    for i, j in _sort16_network():
        exchange(i, j)
    shift = SUBLANES // 2
    while shift >= 1:
        other = [pltpu.roll(x, shift, axis=0) for x in v]
        v = [jnp.maximum(v[i], other[n - 1 - i]) for i in range(n)]
        dist = n // 2
        while dist >= 1:
            for i in range(n):
                if i & dist == 0:
                    exchange(i, i + dist)
            dist //= 2
        shift //= 2
    return [x[0:1, :] for x in v]


def _pair_candidates(t1, t2):
    rows = [t1[a] + t2[b] for a in range(PEER_TOPK) for b in range(PEER_TOPK // (a + 1))]
    pad = -len(rows) % 8
    rows += [jnp.full_like(rows[0], NEG_INF)] * pad
    return jnp.concatenate(rows, axis=0)


def _peer_kernel(xm_ref, xt_ref, g_ref, wq_ref, keys_ref, u_ref, vt_ref, o_ref,
                 h_scr, q_scr, th_scr, s2_scr, e2_scr, c1_scr, acc_scr, *, n_main):
    c = pl.program_id(1)
    tb = xm_ref.shape[0]

    def tokens():
        return jnp.where(pl.program_id(0) < n_main, xm_ref[...], xt_ref[...])

    @pl.when(c == 0)
    def _():
        hb = _rms(tokens(), g_ref[...]).astype(BF16)
        h_scr[...] = hb
        q_scr[...] = _dot_nt(wq_ref[...], hb)

        def head(hd, carry):
            base = pl.multiple_of(hd * (2 * PEER_HALF), 2 * PEER_HALF)
            q1 = q_scr[pl.ds(base, PEER_HALF), :].astype(BF16)
            q2 = q_scr[pl.ds(base + PEER_HALF, PEER_HALF), :].astype(BF16)
            s1 = _dot(keys_ref[hd, 0], q1)
            s2 = _dot(keys_ref[hd, 1], q2)
            t1 = _top16_sorted(s1)
            t2 = _top16_sorted(s2)
            cand = _pair_candidates(t1, t2)
            tau = _top_values(cand, PEER_TOPK)[-1]
            top = t1[0] + t2[0]
            z = jnp.sum(jnp.where(cand >= tau, jnp.exp(cand - top), 0.0), axis=0, keepdims=True)
            c1 = jnp.exp(s1 - t1[0]) * (1.0 / z)
            e2 = jnp.exp(s2 - t2[0])
            th = jnp.full(s1.shape, jnp.inf, F32)
            for a in range(PEER_TOPK):
                th_a = jnp.full_like(tau, jnp.inf)
                for b in range(PEER_TOPK // (a + 1)):
                    th_a = jnp.minimum(th_a, jnp.where(t1[a] + t2[b] >= tau, t2[b], jnp.inf))
                th = jnp.where(s1 == t1[a], th_a, th)
            for lt in range(tb // PEER_LANES):
                ls = slice(lt * PEER_LANES, (lt + 1) * PEER_LANES)
                th_scr[hd, lt] = th[:, ls]
                s2_scr[hd, lt] = s2[:, ls]
                c1_scr[hd, lt] = c1[:, ls]
                e2_scr[hd, lt] = e2[:, ls]
            return carry

        lax.fori_loop(0, PEER_HEADS, head, 0, unroll=4)
        acc_scr[...] = jnp.zeros_like(acc_scr)

    hb = h_scr[...]
    rows_per_sub = PEER_SUB // N_KEYS
    for r in range(PEER_CHUNK // PEER_SUB):
        ga = _gelu(_dot_nt(u_ref[r * PEER_SUB:(r + 1) * PEER_SUB, :], hb))
        w_rows = []
        for kr in range(rows_per_sub):
            i1 = c * PEER_ROWS + r * rows_per_sub + kr
            cols = []
            for lt in range(tb // PEER_LANES):
                g = None
                for hd in range(PEER_HEADS):
                    keep = s2_scr[hd, lt] >= th_scr[hd, lt, pl.ds(i1, 1), :]
                    term = jnp.where(keep, e2_scr[hd, lt], 0.0) * c1_scr[hd, lt, pl.ds(i1, 1), :]
                    g = term if g is None else g + term
                ls = slice(lt * PEER_LANES, (lt + 1) * PEER_LANES)
                cols.append((g * ga[kr * N_KEYS:(kr + 1) * N_KEYS, ls]).astype(BF16))
            w_rows.append(jnp.concatenate(cols, axis=1))
        w = jnp.concatenate(w_rows, axis=0)
        acc_scr[...] += _dot(vt_ref[:, r * PEER_SUB:(r + 1) * PEER_SUB], w)

    @pl.when(c == pl.num_programs(1) - 1)
    def _():
        o_ref[...] = tokens() + acc_scr[...].T


def _peer(x_main, n_main_rows, x_tail, g, wq_t, keys, u, v_t):
    d = x_main.shape[1]
    tb = PEER_TOKENS
    assert n_main_rows % tb == 0 and x_tail.shape[0] % tb == 0
    n_main = n_main_rows // tb
    t = n_main_rows + x_tail.shape[0]
    n_exp = u.shape[0]
    row = pl.BlockSpec((tb, d), lambda i, c: (i, 0))
    tile_shape = (PEER_HEADS, tb // PEER_LANES, N_KEYS, PEER_LANES)
    return pl.pallas_call(
        functools.partial(_peer_kernel, n_main=n_main),
        grid=(t // tb, n_exp // PEER_CHUNK),
        in_specs=[pl.BlockSpec((tb, d), lambda i, c: (jnp.minimum(i, n_main - 1), 0)),
                  pl.BlockSpec((tb, d), lambda i, c: (jnp.maximum(i - n_main, 0), 0)),
                  pl.BlockSpec((1, d), lambda i, c: (0, 0)),
                  pl.BlockSpec(wq_t.shape, lambda i, c: (0, 0)),
                  pl.BlockSpec(keys.shape, lambda i, c: (0, 0, 0, 0)),
                  pl.BlockSpec((PEER_CHUNK, d), lambda i, c: (c, 0)),
                  pl.BlockSpec((d, PEER_CHUNK), lambda i, c: (0, c))],
        out_specs=row,
        out_shape=jax.ShapeDtypeStruct((t, d), F32),
        scratch_shapes=[pltpu.VMEM((tb, d), BF16),
                        pltpu.VMEM((PEER_HEADS * 2 * PEER_HALF, tb), F32),
                        pltpu.VMEM(tile_shape, F32),
                        pltpu.VMEM(tile_shape, F32),
                        pltpu.VMEM(tile_shape, F32),
                        pltpu.VMEM(tile_shape, F32),
                        pltpu.VMEM((d, tb), F32)],
        compiler_params=_cparams(("parallel", "arbitrary")),
        name="peer",
    )(x_main, x_tail, g.reshape(1, d), wq_t, keys, u, v_t)


def _final_norm_kernel(x_ref, g_ref, y_ref):
    y_ref[...] = _rms(x_ref[...], g_ref[...])


def _final_norm(x, g):
    t, d = x.shape
    row = pl.BlockSpec((ROW_TILE, d), lambda i: (i, 0))
    return pl.pallas_call(
        _final_norm_kernel,
        grid=(t // ROW_TILE,),
        in_specs=[row, pl.BlockSpec((1, d), lambda i: (0, 0))],
        out_specs=row,
        out_shape=jax.ShapeDtypeStruct((t, d), F32),
        compiler_params=_cparams(("parallel",)),
        name="final_norm",
    )(x, g.reshape(1, d))


def kernel(x_prompt, x_sample, cache_k, cache_v, state_conv, page_table, meta_tokens, g_mix, g_ffn, g_final, w_qkv, lambda_q1, lambda_k1, lambda_q2, lambda_k2, subln_g, w_o_attn, w_pw1, b_pw1, w_dw, b_dw, ln_g, ln_b, w_pw2, peer_w_query, peer_sub_keys, peer_u, peer_v):
    n_batch, seq, d = x_prompt.shape
    n_dec = x_sample.shape[0]
    depth = g_mix.shape[0]
    seq_len = seq + N_META
    seq_pad = -(-seq_len // ATT_TILE) * ATT_TILE
    n_prompt_rows = n_batch * seq_pad
    assert seq_pad % CONV_TILE == 0 and n_prompt_rows % SAMPLE_CONV_SEQS == 0
    assert x_sample.shape[1] == 1 and n_dec % SAMPLE_CONV_SEQS == 0
    t_total = -(-(n_prompt_rows + n_dec) // ROW_TILE) * ROW_TILE
    assert t_total % PEER_TOKENS == 0
    n_tail = t_total - n_prompt_rows - n_dec

    def tail_tile(sample_rows):
        return jnp.concatenate([sample_rows, jnp.zeros((n_tail, d), sample_rows.dtype)], axis=0)

    def with_sample_rows(prompt_rows, sample_rows):
        return jnp.concatenate([prompt_rows, tail_tile(sample_rows)], axis=0)

    meta = jnp.broadcast_to(meta_tokens.astype(F32)[None], (n_batch, N_META, d))
    xp = jnp.concatenate([meta, x_prompt, jnp.zeros((n_batch, seq_pad - seq_len, d), F32)], axis=1)
    x = with_sample_rows(xp.reshape(n_prompt_rows, d), x_sample.reshape(n_dec, d))

    def prompt_part(a):
        return a[:n_prompt_rows].reshape(n_batch, seq_pad, -1)[:, :seq_len]

    def sample_part(a):
        return a[n_prompt_rows:n_prompt_rows + n_dec]

    kp_l, vp_l, cp_l, ks_l, vs_l, cs_l = [], [], [], [], [], []
    for i in range(depth):
        j = i // 2
        if i % 2 == 0:
            lam_init = 0.8 - 0.6 * math.exp(-0.3 * i)
            lam_params = jnp.stack([lambda_q1[j], lambda_k1[j], lambda_q2[j], lambda_k2[j]])
            qs, kb, vb, kf, vf = _qkv(x, g_mix[i], w_qkv[j].astype(BF16))
            o_p = _prompt_attn(qs, kb, vb, lam_params, subln_g[j], lam_init, n_batch, seq_pad)
            heads = (n_dec, N_HEADS, V_HEAD_DIM)
            o_s = _sample_attn(page_table, sample_part(qs).astype(F32).reshape(heads),
                               sample_part(kf).reshape(heads), sample_part(vf).reshape(heads),
                               cache_k, cache_v, j, lam_params, subln_g[j], lam_init)
            x = _oproj(o_p, tail_tile(o_s.reshape(n_dec, d).astype(BF16)), w_o_attn[j].astype(BF16), x)
            x_main, x_tail = x, x[n_prompt_rows:]
            kv_shape = (n_batch, seq_len, N_HEADS, V_HEAD_DIM)
            kp_l.append(prompt_part(kf).reshape(kv_shape))
            vp_l.append(prompt_part(vf).reshape(kv_shape))
            ks_l.append(sample_part(kf).reshape(n_dec, 1, N_HEADS, V_HEAD_DIM))
            vs_l.append(sample_part(vf).reshape(n_dec, 1, N_HEADS, V_HEAD_DIM))
        else:
            u = _glu(x, g_mix[i], w_pw1[j].astype(BF16), b_pw1[j])
            w2 = w_pw2[j].astype(BF16)
            x_p = _conv_prompt(u, x, w_dw[j], b_dw[j], ln_g[j], ln_b[j], w2, n_prompt_rows, seq_pad)
            x_s = _conv_sample(state_conv[j], u, x, w_dw[j], b_dw[j], ln_g[j], ln_b[j], w2, n_prompt_rows)
            x_main, x_tail = x_p, tail_tile(x_s)
            cp_l.append(prompt_part(u)[:, seq_len - CONV_STATE:])
            cs_l.append(jnp.concatenate([state_conv[j][:, 1:], sample_part(u)[:, None]], axis=1))
        x = _peer(x_main, n_prompt_rows, x_tail, g_ffn[i], peer_w_query[i].T.astype(BF16),
                  peer_sub_keys[i].astype(BF16), peer_u[i].astype(BF16), peer_v[i].T.astype(BF16))
    y = _final_norm(x, g_final)
    return (prompt_part(y)[:, N_META:], sample_part(y).reshape(n_dec, 1, d),
            jnp.stack(kp_l), jnp.stack(vp_l), jnp.stack(cp_l),
            jnp.stack(ks_l), jnp.stack(vs_l), jnp.stack(cs_l))
```

```python
import functools
import math

import jax
import jax.numpy as jnp
from jax import lax
from jax.experimental import pallas as pl
from jax.experimental.pallas import tpu as pltpu

F32 = jnp.float32
BF16 = jnp.bfloat16

D_MODEL = 1024
N_META = 16
N_HEADS = 8
HEAD_DIM = 64
V_HEAD_DIM = 2 * HEAD_DIM
CONV_WIDTH = 31
CONV_STATE = CONV_WIDTH - 1
PAGE_SIZE = 128
PEER_HEADS = 8
N_KEYS = 128
PEER_TOPK = 16
PEER_HALF = 128
RMS_EPS = 1e-6
LN_EPS = 1e-5

SUBLANES = 8
LANES = 128
VMEM_LIMIT_BYTES = 56 * 1024 * 1024

ROW_TILE = 512
ATT_TILE = 384
ATT_Q = 384
ATT_K = 384
CONV_TILE = 384
CONV_HALO = 32
PEER_TOKENS = 256
PEER_ROWS = 16
PEER_CHUNK = PEER_ROWS * N_KEYS
PEER_SUB = 2 * N_KEYS
PEER_LANES = 128
SAMPLE_CONV_SEQS = 16
SAMPLE_PAGES = 8

NEG_INF = float("-inf")


def _cparams(sem):
    return pltpu.CompilerParams(dimension_semantics=sem, vmem_limit_bytes=VMEM_LIMIT_BYTES)


def _rms(x, g):
    return x * lax.rsqrt(jnp.mean(x * x, axis=-1, keepdims=True) + RMS_EPS) * g


def _gelu(x):
    return 0.5 * x * (1.0 + lax.erf(x * math.sqrt(0.5)))


def _dot(a, b):
    return jnp.dot(a, b, preferred_element_type=F32)


def _dot_nt(a, b):
    return lax.dot_general(a, b, (((1,), (1,)), ((), ())), preferred_element_type=F32)


def _qkv_kernel(x_ref, g_ref, w_ref, q_ref, kb_ref, vb_ref, kf_ref, vf_ref):
    d = x_ref.shape[1]
    hb = _rms(x_ref[...], g_ref[...]).astype(BF16)
    q = _dot(hb, w_ref[:, 0:d])
    q_ref[...] = (q * (HEAD_DIM ** -0.5)).astype(BF16)
    k = _dot(hb, w_ref[:, d:2 * d])
    kf_ref[...] = k
    kb_ref[...] = k.astype(BF16)
    v = _dot(hb, w_ref[:, 2 * d:3 * d])
    vf_ref[...] = v
    vb_ref[...] = v.astype(BF16)


def _qkv(x, g, w):
    t, d = x.shape
    row = pl.BlockSpec((ROW_TILE, d), lambda i: (i, 0))
    return pl.pallas_call(
        _qkv_kernel,
        grid=(t // ROW_TILE,),
        in_specs=[row, pl.BlockSpec((1, d), lambda i: (0, 0)), pl.BlockSpec((d, 3 * d), lambda i: (0, 0))],
        out_specs=[row] * 5,
        out_shape=[jax.ShapeDtypeStruct((t, d), BF16)] * 3 + [jax.ShapeDtypeStruct((t, d), F32)] * 2,
        compiler_params=_cparams(("parallel",)),
        name="qkv",
    )(x, g.reshape(1, d), w)


def _lambda(lam_ref, lam_init):
    p = lam_ref[...]
    a = jnp.sum(p[0:1] * p[1:2], axis=-1, keepdims=True)
    b = jnp.sum(p[2:3] * p[3:4], axis=-1, keepdims=True)
    return jnp.exp(a) - jnp.exp(b) + lam_init


def _subln(o, g, lam_init):
    return _rms(o, g) * (1.0 - lam_init)


def _prompt_attn_kernel(q_ref, k_ref, v_ref, lam_ref, sg_ref, o_ref, *, lam_init):
    qi = pl.program_id(2)
    tq = q_ref.shape[0]
    tk = ATT_K
    q = q_ref[...]
    lane = lax.broadcasted_iota(jnp.int32, q.shape, 1)
    zero = jnp.zeros_like(q)
    qq = jnp.concatenate([jnp.where(lane < HEAD_DIM, q, zero), jnp.where(lane >= HEAD_DIM, q, zero)], axis=0)
    q0 = qi * tq

    def step(j, carry, masked):
        m, l, acc = carry
        start = pl.multiple_of(j * tk, tk)
        k = k_ref[pl.ds(start, tk), :]
        v = v_ref[pl.ds(start, tk), :]
        s = _dot_nt(qq, k)
        if masked:
            r = q0 + lax.broadcasted_iota(jnp.int32, (tq, tk), 0)
            c = start + lax.broadcasted_iota(jnp.int32, (tq, tk), 1)
            keep = jnp.concatenate([c <= r, c <= r], axis=0)
            s = jnp.where(keep, s, NEG_INF)
        m_new = jnp.maximum(m, jnp.max(s, axis=-1, keepdims=True))
        alpha = jnp.exp(m - m_new)
        p = jnp.exp(s - m_new)
        l = alpha * l + jnp.sum(p, axis=-1, keepdims=True)
        acc = alpha * acc + _dot(p.astype(BF16), v)
        return m_new, l, acc

    init = (jnp.full((2 * tq, 1), NEG_INF, F32), jnp.zeros((2 * tq, 1), F32), jnp.zeros((2 * tq, V_HEAD_DIM), F32))
    n_full = q0 // tk
    carry = lax.fori_loop(0, n_full // 2, lambda j, c: step(2 * j + 1, step(2 * j, c, False), False), init)
    carry = lax.cond(n_full % 2 == 1, lambda c: step(n_full - 1, c, False), lambda c: c, carry)
    for t in range(max(1, tq // tk)):
        carry = step(n_full + t, carry, True)
    m, l, acc = carry
    o = acc / l
    lam = _lambda(lam_ref, lam_init)
    o = o[:tq] - lam * o[tq:]
    o_ref[...] = _subln(o, sg_ref[...], lam_init).astype(o_ref.dtype)


def _prompt_attn(q, k, v, lam_params, subln_g, lam_init, n_batch, seq_pad):
    d = q.shape[1]
    assert seq_pad % ATT_Q == 0 and seq_pad % ATT_K == 0 and (ATT_K % ATT_Q == 0 or ATT_Q % ATT_K == 0)
    nq = seq_pad // ATT_Q
    qspec = pl.BlockSpec((ATT_Q, V_HEAD_DIM), lambda b, h, i: (b * nq + i, h))
    kvspec = pl.BlockSpec((seq_pad, V_HEAD_DIM), lambda b, h, i: (b, h))
    return pl.pallas_call(
        functools.partial(_prompt_attn_kernel, lam_init=lam_init),
        grid=(n_batch, N_HEADS, nq),
        in_specs=[qspec, kvspec, kvspec,
                  pl.BlockSpec((4, HEAD_DIM), lambda b, h, i: (0, 0)),
                  pl.BlockSpec((1, V_HEAD_DIM), lambda b, h, i: (0, 0))],
        out_specs=qspec,
        out_shape=jax.ShapeDtypeStruct((n_batch * seq_pad, d), BF16),
        compiler_params=_cparams(("parallel", "parallel", "arbitrary")),
        name="prompt_attn",
    )(q, k, v, lam_params, subln_g.reshape(1, V_HEAD_DIM))


def _sample_attn_kernel(pt_ref, q_ref, kn_ref, vn_ref, *refs, lam_init):
    k_refs = refs[:SAMPLE_PAGES]
    v_refs = refs[SAMPLE_PAGES:2 * SAMPLE_PAGES]
    ones_ref, lam_ref, sg_ref, o_ref, m_scr, l_scr, acc1_scr, acc2_scr = refs[2 * SAMPLE_PAGES:]
    p = pl.program_id(1)
    q = q_ref[0]
    ones = ones_ref[...]

    def spread(x):
        first = lax.broadcasted_iota(jnp.int32, x.shape, x.ndim - 1) < HEAD_DIM
        swapped = pltpu.roll(x, HEAD_DIM, axis=x.ndim - 1)
        return jnp.where(first, x, swapped), jnp.where(first, swapped, x)

    @pl.when(p == 0)
    def _():
        m_scr[...] = _dot((kn_ref[0] * q).astype(BF16), ones)
        l_scr[...] = jnp.ones(l_scr.shape, F32)
        acc1_scr[...] = vn_ref[0]
        acc2_scr[...] = vn_ref[0]

    scores = []
    for k_ref in k_refs:
        kp = k_ref[...]
        n_tok = kp.shape[0]
        prod = (kp * q[None]).reshape(n_tok * N_HEADS, V_HEAD_DIM).astype(BF16)
        scores.append(_dot(prod, ones).reshape(n_tok, N_HEADS, V_HEAD_DIM))
    m_old = m_scr[...]
    m_new = m_old
    for s in scores:
        m_new = jnp.maximum(m_new, jnp.max(s, axis=0))
    alpha = jnp.exp(m_old - m_new)
    l = alpha * l_scr[...]
    alpha1, alpha2 = spread(alpha)
    acc1 = alpha1 * acc1_scr[...]
    acc2 = alpha2 * acc2_scr[...]
    for s, v_ref in zip(scores, v_refs):
        pe = jnp.exp(s - m_new[None])
        l = l + jnp.sum(pe, axis=0)
        p1, p2 = spread(pe)
        vp = v_ref[...]
        acc1 = acc1 + jnp.sum(p1 * vp, axis=0)
        acc2 = acc2 + jnp.sum(p2 * vp, axis=0)
    l_scr[...] = l
    acc1_scr[...] = acc1
    acc2_scr[...] = acc2
    m_scr[...] = m_new

    @pl.when(p == pl.num_programs(1) - 1)
    def _():
        l1, l2 = spread(l_scr[...])
        lam = _lambda(lam_ref, lam_init)
        o = acc1_scr[...] / l1 - lam * (acc2_scr[...] / l2)
        o_ref[0] = _subln(o, sg_ref[...], lam_init)


def _sample_attn(page_table, q, k_new, v_new, k_pool, v_pool, layer, lam_params, subln_g, lam_init):
    n_seq, n_pages = page_table.shape
    assert n_pages % SAMPLE_PAGES == 0
    hd = (N_HEADS, V_HEAD_DIM)
    half = jnp.arange(V_HEAD_DIM) >= HEAD_DIM
    ones = (half[:, None] == half[None, :]).astype(BF16)
    tok = pl.BlockSpec((1,) + hd, lambda s, p, pt: (s, 0, 0))

    def page(k):
        return pl.BlockSpec((None, None, PAGE_SIZE) + hd,
                            lambda s, p, pt: (layer, pt[s, p * SAMPLE_PAGES + k], 0, 0, 0))

    pages = [page(k) for k in range(SAMPLE_PAGES)]
    grid_spec = pltpu.PrefetchScalarGridSpec(
        num_scalar_prefetch=1,
        grid=(n_seq, n_pages // SAMPLE_PAGES),
        in_specs=[tok, tok, tok] + pages + pages +
                 [pl.BlockSpec((V_HEAD_DIM, V_HEAD_DIM), lambda s, p, pt: (0, 0)),
                  pl.BlockSpec((4, HEAD_DIM), lambda s, p, pt: (0, 0)),
                  pl.BlockSpec((1, V_HEAD_DIM), lambda s, p, pt: (0, 0))],
        out_specs=tok,
        scratch_shapes=[pltpu.VMEM((N_HEADS, V_HEAD_DIM), F32)] * 4,
    )
    return pl.pallas_call(
        functools.partial(_sample_attn_kernel, lam_init=lam_init),
        grid_spec=grid_spec,
        out_shape=jax.ShapeDtypeStruct((n_seq,) + hd, F32),
        compiler_params=_cparams(("parallel", "arbitrary")),
        name="sample_attn",
    )(page_table, q, k_new, v_new, *([k_pool] * SAMPLE_PAGES), *([v_pool] * SAMPLE_PAGES),
      ones, lam_params, subln_g.reshape(1, V_HEAD_DIM))


def _oproj_kernel(om_ref, ot_ref, w_ref, x_ref, out_ref, *, n_main):
    o = jnp.where(pl.program_id(0) < n_main, om_ref[...], ot_ref[...])
    out_ref[...] = x_ref[...] + _dot(o, w_ref[...])


def _oproj(o_main, o_tail, w, x):
    t, d = x.shape
    n_main = o_main.shape[0] // ROW_TILE
    assert o_main.shape[0] % ROW_TILE == 0 and o_tail.shape[0] == ROW_TILE and t == (n_main + 1) * ROW_TILE
    row = pl.BlockSpec((ROW_TILE, d), lambda i: (i, 0))
    return pl.pallas_call(
        functools.partial(_oproj_kernel, n_main=n_main),
        grid=(t // ROW_TILE,),
        in_specs=[pl.BlockSpec((ROW_TILE, d), lambda i: (jnp.minimum(i, n_main - 1), 0)),
                  pl.BlockSpec((ROW_TILE, d), lambda i: (0, 0)),
                  pl.BlockSpec((d, d), lambda i: (0, 0)), row],
        out_specs=row,
        out_shape=jax.ShapeDtypeStruct((t, d), F32),
        compiler_params=_cparams(("parallel",)),
        name="oproj",
    )(o_main, o_tail, w, x)


def _glu_kernel(x_ref, g_ref, w_ref, b_ref, u_ref):
    d = x_ref.shape[1]
    hb = _rms(x_ref[...], g_ref[...]).astype(BF16)
    a = _dot(hb, w_ref[:, 0:d]) + b_ref[:, 0:d]
    gate = _dot(hb, w_ref[:, d:2 * d]) + b_ref[:, d:2 * d]
    u_ref[...] = a * jax.nn.sigmoid(gate)


def _glu(x, g, w, b):
    t, d = x.shape
    row = pl.BlockSpec((ROW_TILE, d), lambda i: (i, 0))
    return pl.pallas_call(
        _glu_kernel,
        grid=(t // ROW_TILE,),
        in_specs=[row, pl.BlockSpec((1, d), lambda i: (0, 0)), pl.BlockSpec((d, 2 * d), lambda i: (0, 0)),
                  pl.BlockSpec((1, 2 * d), lambda i: (0, 0))],
        out_specs=row,
        out_shape=jax.ShapeDtypeStruct((t, d), F32),
        compiler_params=_cparams(("parallel",)),
        name="glu",
    )(x, g.reshape(1, d), w, b.reshape(1, 2 * d))


def _conv_tail(c, lng_ref, lnb_ref, w2_ref, x_ref, out_ref):
    mu = jnp.mean(c, axis=-1, keepdims=True)
    cc = c - mu
    var = jnp.mean(cc * cc, axis=-1, keepdims=True)
    cn = cc * lax.rsqrt(var + LN_EPS) * lng_ref[...] + lnb_ref[...]
    act = (cn * jax.nn.sigmoid(cn)).astype(BF16)
    out_ref[...] = x_ref[...] + _dot(act, w2_ref[...])


def _conv_prompt_kernel(u_ref, halo_ref, wdw_ref, bdw_ref, lng_ref, lnb_ref, w2_ref, x_ref, out_ref,
                        buf, shift_scr, c_scr, *, tiles_per_seq):
    i = pl.program_id(0)
    tile, d = u_ref.shape
    halo = halo_ref[...]
    buf[0:CONV_HALO, :] = jnp.where(i % tiles_per_seq == 0, jnp.zeros_like(halo), halo)
    buf[CONV_HALO:CONV_HALO + tile, :] = u_ref[...]
    n_shift = shift_scr.shape[1]
    for s in range(1, SUBLANES):
        shift_scr[s - 1] = buf[s:s + n_shift, :]
    off = CONV_HALO - CONV_STATE
    for l0 in range(0, d, LANES):
        ls = slice(l0, l0 + LANES)
        c = None
        for w in range(CONV_WIDTH):
            s = (off + w) % SUBLANES
            a = off + w - s
            rows = buf[a:a + tile, ls] if s == 0 else shift_scr[s - 1, a:a + tile, ls]
            term = rows * wdw_ref[w:w + 1, ls]
            c = term if c is None else c + term
        c_scr[:, ls] = c + bdw_ref[:, ls]
    _conv_tail(c_scr[...], lng_ref, lnb_ref, w2_ref, x_ref, out_ref)


def _conv_prompt(u, x, w_dw, b_dw, ln_g, ln_b, w2, n_rows, seq_pad):
    d = u.shape[1]
    per_tile = CONV_TILE // CONV_HALO
    row = pl.BlockSpec((CONV_TILE, d), lambda i: (i, 0))
    vec = pl.BlockSpec((1, d), lambda i: (0, 0))
    return pl.pallas_call(
        functools.partial(_conv_prompt_kernel, tiles_per_seq=seq_pad // CONV_TILE),
        grid=(n_rows // CONV_TILE,),
        in_specs=[row, pl.BlockSpec((CONV_HALO, d), lambda i: (jnp.maximum(i * per_tile - 1, 0), 0)),
                  pl.BlockSpec((CONV_WIDTH, d), lambda i: (0, 0)), vec, vec, vec,
                  pl.BlockSpec((d, d), lambda i: (0, 0)), row],
        out_specs=row,
        out_shape=jax.ShapeDtypeStruct((n_rows, d), F32),
        scratch_shapes=[pltpu.VMEM((CONV_HALO + CONV_TILE, d), F32),
                        pltpu.VMEM((SUBLANES - 1, CONV_HALO + CONV_TILE - SUBLANES, d), F32),
                        pltpu.VMEM((CONV_TILE, d), F32)],
        compiler_params=_cparams(("parallel",)),
        name="conv_prompt",
    )(u, u, w_dw, b_dw.reshape(1, d), ln_g.reshape(1, d), ln_b.reshape(1, d), w2, x)


def _conv_sample_kernel(st_ref, u_ref, wdw_ref, bdw_ref, lng_ref, lnb_ref, w2_ref, x_ref, out_ref):
    c = jnp.sum(st_ref[...] * wdw_ref[0:CONV_STATE, :][None], axis=1)
    c = c + u_ref[...] * wdw_ref[CONV_STATE:CONV_WIDTH, :] + bdw_ref[...]
    _conv_tail(c, lng_ref, lnb_ref, w2_ref, x_ref, out_ref)


def _conv_sample(state, u, x, w_dw, b_dw, ln_g, ln_b, w2, first_row):
    n_seq, _, d = state.shape
    ns = SAMPLE_CONV_SEQS
    base = first_row // ns
    row = pl.BlockSpec((ns, d), lambda i: (base + i, 0))
    vec = pl.BlockSpec((1, d), lambda i: (0, 0))
    return pl.pallas_call(
        _conv_sample_kernel,
        grid=(n_seq // ns,),
        in_specs=[pl.BlockSpec((ns, CONV_STATE, d), lambda i: (i, 0, 0)), row,
                  pl.BlockSpec((CONV_WIDTH, d), lambda i: (0, 0)), vec, vec, vec,
                  pl.BlockSpec((d, d), lambda i: (0, 0)), row],
        out_specs=pl.BlockSpec((ns, d), lambda i: (i, 0)),
        out_shape=jax.ShapeDtypeStruct((n_seq, d), F32),
        compiler_params=_cparams(("parallel",)),
        name="conv_sample",
    )(state, u, w_dw, b_dw.reshape(1, d), ln_g.reshape(1, d), ln_b.reshape(1, d), w2, x)


def _top_values(s, n):
    rows = []
    work = s
    for r in range(n):
        m = jnp.max(work, axis=0, keepdims=True)
        rows.append(m)
        if r + 1 < n:
            work = jnp.where(work == m, NEG_INF, work)
    return rows


def _sort16_network():
    def merge(lo, hi, r):
        step = r * 2
        if step < hi - lo:
            yield from merge(lo, hi, step)
            yield from merge(lo + r, hi, step)
            yield from [(i, i + r) for i in range(lo + r, hi - r, step)]
        else:
            yield (lo, lo + r)

    def sort(lo, hi):
        if hi - lo >= 1:
            mid = lo + (hi - lo) // 2
            yield from sort(lo, mid)
            yield from sort(mid + 1, hi)
            yield from merge(lo, hi, 1)

    return tuple(sort(0, PEER_TOPK - 1))


def _top16_sorted(s):
    n = PEER_TOPK
    assert s.shape[0] == n * SUBLANES
    v = [s[SUBLANES * k:SUBLANES * (k + 1), :] for k in range(n)]

    def exchange(i, j):
        v[i], v[j] = jnp.maximum(v[i], v[j]), jnp.minimum(v[i], v[j])

    for i, j in _sort16_network():
        exchange(i, j)
    shift = SUBLANES // 2
    while shift >= 1:
        other = [pltpu.roll(x, shift, axis=0) for x in v]
        v = [jnp.maximum(v[i], other[n - 1 - i]) for i in range(n)]
        dist = n // 2
        while dist >= 1:
            for i in range(n):
                if i & dist == 0:
                    exchange(i, i + dist)
            dist //= 2
        shift //= 2
    return [x[0:1, :] for x in v]


def _pair_candidates(t1, t2):
    rows = [t1[a] + t2[b] for a in range(PEER_TOPK) for b in range(PEER_TOPK // (a + 1))]
    pad = -len(rows) % 8
    rows += [jnp.full_like(rows[0], NEG_INF)] * pad
    return jnp.concatenate(rows, axis=0)


def _peer_kernel(xm_ref, xt_ref, g_ref, wq_ref, keys_ref, u_ref, vt_ref, o_ref,
                 h_scr, q_scr, th_scr, s2_scr, e2_scr, c1_scr, acc_scr, *, n_main):
    c = pl.program_id(1)
    tb = xm_ref.shape[0]

    def tokens():
        return jnp.where(pl.program_id(0) < n_main, xm_ref[...], xt_ref[...])

    @pl.when(c == 0)
    def _():
        hb = _rms(tokens(), g_ref[...]).astype(BF16)
        h_scr[...] = hb
        q_scr[...] = _dot_nt(wq_ref[...], hb)

        def head(hd, carry):
            base = pl.multiple_of(hd * (2 * PEER_HALF), 2 * PEER_HALF)
            q1 = q_scr[pl.ds(base, PEER_HALF), :].astype(BF16)
            q2 = q_scr[pl.ds(base + PEER_HALF, PEER_HALF), :].astype(BF16)
            s1 = _dot(keys_ref[hd, 0], q1)
            s2 = _dot(keys_ref[hd, 1], q2)
            t1 = _top16_sorted(s1)
            t2 = _top16_sorted(s2)
            cand = _pair_candidates(t1, t2)
            tau = _top_values(cand, PEER_TOPK)[-1]
            top = t1[0] + t2[0]
            z = jnp.sum(jnp.where(cand >= tau, jnp.exp(cand - top), 0.0), axis=0, keepdims=True)
            c1 = jnp.exp(s1 - t1[0]) * (1.0 / z)
            e2 = jnp.exp(s2 - t2[0])
            th = jnp.full(s1.shape, jnp.inf, F32)
            for a in range(PEER_TOPK):
                th_a = jnp.full_like(tau, jnp.inf)
                for b in range(PEER_TOPK // (a + 1)):
                    th_a = jnp.minimum(th_a, jnp.where(t1[a] + t2[b] >= tau, t2[b], jnp.inf))
                th = jnp.where(s1 == t1[a], th_a, th)
            for lt in range(tb // PEER_LANES):
                ls = slice(lt * PEER_LANES, (lt + 1) * PEER_LANES)
                th_scr[hd, lt] = th[:, ls]
                s2_scr[hd, lt] = s2[:, ls]
                c1_scr[hd, lt] = c1[:, ls]
                e2_scr[hd, lt] = e2[:, ls]
            return carry

        lax.fori_loop(0, PEER_HEADS, head, 0, unroll=4)
        acc_scr[...] = jnp.zeros_like(acc_scr)

    hb = h_scr[...]
    rows_per_sub = PEER_SUB // N_KEYS
    for r in range(PEER_CHUNK // PEER_SUB):
        ga = _gelu(_dot_nt(u_ref[r * PEER_SUB:(r + 1) * PEER_SUB, :], hb))
        w_rows = []
        for kr in range(rows_per_sub):
            i1 = c * PEER_ROWS + r * rows_per_sub + kr
            cols = []
            for lt in range(tb // PEER_LANES):
                g = None
                for hd in range(PEER_HEADS):
                    keep = s2_scr[hd, lt] >= th_scr[hd, lt, pl.ds(i1, 1), :]
                    term = jnp.where(keep, e2_scr[hd, lt], 0.0) * c1_scr[hd, lt, pl.ds(i1, 1), :]
                    g = term if g is None else g + term
                ls = slice(lt * PEER_LANES, (lt + 1) * PEER_LANES)
                cols.append((g * ga[kr * N_KEYS:(kr + 1) * N_KEYS, ls]).astype(BF16))
            w_rows.append(jnp.concatenate(cols, axis=1))
        w = jnp.concatenate(w_rows, axis=0)
        acc_scr[...] += _dot(vt_ref[:, r * PEER_SUB:(r + 1) * PEER_SUB], w)

    @pl.when(c == pl.num_programs(1) - 1)
    def _():
        o_ref[...] = tokens() + acc_scr[...].T


def _peer(x_main, n_main_rows, x_tail, g, wq_t, keys, u, v_t):
    d = x_main.shape[1]
    tb = PEER_TOKENS
    assert n_main_rows % tb == 0 and x_tail.shape[0] % tb == 0
    n_main = n_main_rows // tb
    t = n_main_rows + x_tail.shape[0]
    n_exp = u.shape[0]
    row = pl.BlockSpec((tb, d), lambda i, c: (i, 0))
    tile_shape = (PEER_HEADS, tb // PEER_LANES, N_KEYS, PEER_LANES)
    return pl.pallas_call(
        functools.partial(_peer_kernel, n_main=n_main),
        grid=(t // tb, n_exp // PEER_CHUNK),
        in_specs=[pl.BlockSpec((tb, d), lambda i, c: (jnp.minimum(i, n_main - 1), 0)),
                  pl.BlockSpec((tb, d), lambda i, c: (jnp.maximum(i - n_main, 0), 0)),
                  pl.BlockSpec((1, d), lambda i, c: (0, 0)),
                  pl.BlockSpec(wq_t.shape, lambda i, c: (0, 0)),
                  pl.BlockSpec(keys.shape, lambda i, c: (0, 0, 0, 0)),
                  pl.BlockSpec((PEER_CHUNK, d), lambda i, c: (c, 0)),
                  pl.BlockSpec((d, PEER_CHUNK), lambda i, c: (0, c))],
        out_specs=row,
        out_shape=jax.ShapeDtypeStruct((t, d), F32),
        scratch_shapes=[pltpu.VMEM((tb, d), BF16),
                        pltpu.VMEM((PEER_HEADS * 2 * PEER_HALF, tb), F32),
                        pltpu.VMEM(tile_shape, F32),
                        pltpu.VMEM(tile_shape, F32),
                        pltpu.VMEM(tile_shape, F32),
                        pltpu.VMEM(tile_shape, F32),
                        pltpu.VMEM((d, tb), F32)],
        compiler_params=_cparams(("parallel", "arbitrary")),
        name="peer",
    )(x_main, x_tail, g.reshape(1, d), wq_t, keys, u, v_t)


def _final_norm_kernel(x_ref, g_ref, y_ref):
    y_ref[...] = _rms(x_ref[...], g_ref[...])


def _final_norm(x, g):
    t, d = x.shape
    row = pl.BlockSpec((ROW_TILE, d), lambda i: (i, 0))
    return pl.pallas_call(
        _final_norm_kernel,
        grid=(t // ROW_TILE,),
        in_specs=[row, pl.BlockSpec((1, d), lambda i: (0, 0))],
        out_specs=row,
        out_shape=jax.ShapeDtypeStruct((t, d), F32),
        compiler_params=_cparams(("parallel",)),
        name="final_norm",
    )(x, g.reshape(1, d))


def kernel(x_prompt, x_sample, cache_k, cache_v, state_conv, page_table, meta_tokens, g_mix, g_ffn, g_final, w_qkv, lambda_q1, lambda_k1, lambda_q2, lambda_k2, subln_g, w_o_attn, w_pw1, b_pw1, w_dw, b_dw, ln_g, ln_b, w_pw2, peer_w_query, peer_sub_keys, peer_u, peer_v):
    n_batch, seq, d = x_prompt.shape
    n_dec = x_sample.shape[0]
    depth = g_mix.shape[0]
    seq_len = seq + N_META
    seq_pad = -(-seq_len // ATT_TILE) * ATT_TILE
    n_prompt_rows = n_batch * seq_pad
    assert seq_pad % CONV_TILE == 0 and n_prompt_rows % SAMPLE_CONV_SEQS == 0
    assert x_sample.shape[1] == 1 and n_dec % SAMPLE_CONV_SEQS == 0
    t_total = -(-(n_prompt_rows + n_dec) // ROW_TILE) * ROW_TILE
    assert t_total % PEER_TOKENS == 0
    n_tail = t_total - n_prompt_rows - n_dec

    def tail_tile(sample_rows):
        return jnp.concatenate([sample_rows, jnp.zeros((n_tail, d), sample_rows.dtype)], axis=0)

    def with_sample_rows(prompt_rows, sample_rows):
        return jnp.concatenate([prompt_rows, tail_tile(sample_rows)], axis=0)

    meta = jnp.broadcast_to(meta_tokens.astype(F32)[None], (n_batch, N_META, d))
    xp = jnp.concatenate([meta, x_prompt, jnp.zeros((n_batch, seq_pad - seq_len, d), F32)], axis=1)
    x = with_sample_rows(xp.reshape(n_prompt_rows, d), x_sample.reshape(n_dec, d))

    def prompt_part(a):
        return a[:n_prompt_rows].reshape(n_batch, seq_pad, -1)[:, :seq_len]

    def sample_part(a):
        return a[n_prompt_rows:n_prompt_rows + n_dec]

    kp_l, vp_l, cp_l, ks_l, vs_l, cs_l = [], [], [], [], [], []
    for i in range(depth):
        j = i // 2
        if i % 2 == 0:
            lam_init = 0.8 - 0.6 * math.exp(-0.3 * i)
            lam_params = jnp.stack([lambda_q1[j], lambda_k1[j], lambda_q2[j], lambda_k2[j]])
            qs, kb, vb, kf, vf = _qkv(x, g_mix[i], w_qkv[j].astype(BF16))
            o_p = _prompt_attn(qs, kb, vb, lam_params, subln_g[j], lam_init, n_batch, seq_pad)
            heads = (n_dec, N_HEADS, V_HEAD_DIM)
            o_s = _sample_attn(page_table, sample_part(qs).astype(F32).reshape(heads),
                               sample_part(kf).reshape(heads), sample_part(vf).reshape(heads),
                               cache_k, cache_v, j, lam_params, subln_g[j], lam_init)
            x = _oproj(o_p, tail_tile(o_s.reshape(n_dec, d).astype(BF16)), w_o_attn[j].astype(BF16), x)
            x_main, x_tail = x, x[n_prompt_rows:]
            kv_shape = (n_batch, seq_len, N_HEADS, V_HEAD_DIM)
            kp_l.append(prompt_part(kf).reshape(kv_shape))
            vp_l.append(prompt_part(vf).reshape(kv_shape))
            ks_l.append(sample_part(kf).reshape(n_dec, 1, N_HEADS, V_HEAD_DIM))
            vs_l.append(sample_part(vf).reshape(n_dec, 1, N_HEADS, V_HEAD_DIM))
        else:
            u = _glu(x, g_mix[i], w_pw1[j].astype(BF16), b_pw1[j])
            w2 = w_pw2[j].astype(BF16)
            x_p = _conv_prompt(u, x, w_dw[j], b_dw[j], ln_g[j], ln_b[j], w2, n_prompt_rows, seq_pad)
            x_s = _conv_sample(state_conv[j], u, x, w_dw[j], b_dw[j], ln_g[j], ln_b[j], w2, n_prompt_rows)
            x_main, x_tail = x_p, tail_tile(x_s)
            cp_l.append(prompt_part(u)[:, seq_len - CONV_STATE:])
            cs_l.append(jnp.concatenate([state_conv[j][:, 1:], sample_part(u)[:, None]], axis=1))
        x = _peer(x_main, n_prompt_rows, x_tail, g_ffn[i], peer_w_query[i].T.astype(BF16),
                  peer_sub_keys[i].astype(BF16), peer_u[i].astype(BF16), peer_v[i].T.astype(BF16))
    y = _final_norm(x, g_final)
    return (prompt_part(y)[:, N_META:], sample_part(y).reshape(n_dec, 1, d),
            jnp.stack(kp_l), jnp.stack(vp_l), jnp.stack(cp_l),
            jnp.stack(ks_l), jnp.stack(vs_l), jnp.stack(cs_l))
```
